```python
import jax, jax.numpy as jnp
from jax import lax
import numpy as np

D_MODEL = 1024
BATCH = 8
SEQ = 4096
DEPTH = 2

N_MIXERS = 2
D_FF = 2816
D_PLE = 256
CONV_WIDTH = 3
RET_HEADS = 4
RET_QK_DIM = D_MODEL // RET_HEADS
RET_V_DIM = 2 * RET_QK_DIM
RET_CHUNK = 128
ROPE_BASE = 10000.0
NORM_EPS = 1e-6
N_CONV_LAYERS = (DEPTH + N_MIXERS - 1) // N_MIXERS
N_RET_LAYERS = DEPTH // N_MIXERS

N_NORMS = 8
FFN1_PRE, FFN1_POST, MIX_PRE, MIX_POST, FFN2_PRE, FFN2_POST, PLE_PRE, PLE_POST = range(N_NORMS)

kernel_name = "hybrid_shortconv_retention_macaron"


def rmsnorm(x, g):
    xf = x.astype(jnp.float32)
    y = xf * lax.rsqrt(jnp.mean(xf * xf, axis=-1, keepdims=True) + NORM_EPS)
    return (y * g.astype(jnp.float32)).astype(x.dtype)


def swiglu(x, w_gate, w_up, w_down):
    return (jax.nn.silu(x @ w_gate) * (x @ w_up)) @ w_down


def short_conv_mixer(x, w_in, conv_w, w_out):
    T = x.shape[1]
    b, c, h = jnp.split(x @ w_in, 3, axis=-1)
    u = c * h
    u_pad = jnp.pad(u, ((0, 0), (CONV_WIDTH - 1, 0), (0, 0)))
    v = conv_w[0] * u_pad[:, 0:T]
    for k in range(1, CONV_WIDTH):
        v = v + conv_w[k] * u_pad[:, k:k + T]
    return (b * v) @ w_out


def rotary(x, pos):
    half = x.shape[-1] // 2
    inv_freq = 1.0 / (ROPE_BASE ** jnp.linspace(0.0, 1.0, half, dtype=jnp.float32))
    ang = pos.astype(jnp.float32)[:, None] * inv_freq[None, :]
    cos = jnp.cos(ang)[:, None, :]
    sin = jnp.sin(ang)[:, None, :]
    x1, x2 = x[..., :half], x[..., half:]
    return jnp.concatenate([x1 * cos - x2 * sin, x2 * cos + x1 * sin], axis=-1)


def retention_mixer(x, w_in, w_out):
    Bsz, T, _ = x.shape
    H, dk, dv, C = RET_HEADS, RET_QK_DIM, RET_V_DIM, RET_CHUNK
    n_chunks = T // C
    f32 = jnp.float32
    proj = x @ w_in
    q, k, v, g = jnp.split(proj, [H * dk, 2 * H * dk, 2 * H * dk + H * dv], axis=-1)
    pos = jnp.arange(T)
    q = rotary(q.reshape(Bsz, T, H, dk).astype(f32), pos)
    k = rotary(k.reshape(Bsz, T, H, dk).astype(f32), pos) * (dk ** -0.5)
    v = v.reshape(Bsz, T, H, dv).astype(f32)

    log_gamma = jnp.log1p(-jnp.exp2(-5.0 - jnp.arange(H, dtype=f32)))
    idx = jnp.arange(C, dtype=f32)
    diff = idx[:, None] - idx[None, :]
    intra_decay = jnp.where(diff >= 0.0,
                            jnp.exp(log_gamma[:, None, None] * jnp.maximum(diff, 0.0)[None]),
                            0.0)
    q_decay = jnp.exp(log_gamma[:, None] * (idx + 1.0))[None, :, :, None]
    k_decay = jnp.exp(log_gamma[:, None] * (C - 1.0 - idx))[None, :, :, None]
    chunk_decay = jnp.exp(log_gamma * C)[None, :, None, None]

    def to_chunks(a):
        return a.reshape(Bsz, n_chunks, C, H, a.shape[-1]).transpose(1, 0, 3, 2, 4)

    def step(state, qkv):
        qc, kc, vc = qkv
        scores = jnp.einsum('bhid,bhjd->bhij', qc, kc) * intra_decay
        inner = jnp.einsum('bhij,bhje->bhie', scores, vc)
        cross = jnp.einsum('bhid,bhde->bhie', qc * q_decay, state)
        new_state = chunk_decay * state + jnp.einsum('bhjd,bhje->bhde', kc * k_decay, vc)
        return new_state, inner + cross

    state0 = jnp.zeros((Bsz, H, dk, dv), f32)
    _, out = lax.scan(step, state0, (to_chunks(q), to_chunks(k), to_chunks(v)))
    out = out.transpose(1, 0, 3, 2, 4).reshape(Bsz, T, H, dv)
    out = out * lax.rsqrt(jnp.mean(out * out, axis=-1, keepdims=True) + NORM_EPS)
    y = jax.nn.silu(g) * out.reshape(Bsz, T, H * dv).astype(x.dtype)
    return y @ w_out


def setup_inputs(seed: int = 0) -> dict:
    key = jax.random.key(seed)
    ks = jax.random.split(key, 16)
    f32 = jnp.float32

    def w(k, shape, fan_in):
        return jax.random.normal(k, shape, f32) * (fan_in ** -0.5)

    ret_in_width = 2 * RET_HEADS * RET_QK_DIM + 2 * RET_HEADS * RET_V_DIM
    return {
        "x": jax.random.normal(ks[0], (BATCH, SEQ, D_MODEL), f32),
        "p": jax.random.normal(ks[1], (DEPTH, BATCH, SEQ, D_PLE), f32),
        "norm_g": 1.0 + 0.05 * jax.random.normal(ks[2], (DEPTH, N_NORMS, D_MODEL), f32),
        "ffn1_w_gate": w(ks[3], (DEPTH, D_MODEL, D_FF), D_MODEL),
        "ffn1_w_up": w(ks[4], (DEPTH, D_MODEL, D_FF), D_MODEL),
        "ffn1_w_down": w(ks[5], (DEPTH, D_FF, D_MODEL), D_FF),
        "ffn2_w_gate": w(ks[6], (DEPTH, D_MODEL, D_FF), D_MODEL),
        "ffn2_w_up": w(ks[7], (DEPTH, D_MODEL, D_FF), D_MODEL),
        "ffn2_w_down": w(ks[8], (DEPTH, D_FF, D_MODEL), D_FF),
        "conv_w_in": w(ks[9], (N_CONV_LAYERS, D_MODEL, 3 * D_MODEL), D_MODEL),
        "conv_w": w(ks[10], (N_CONV_LAYERS, CONV_WIDTH, D_MODEL), CONV_WIDTH),
        "conv_w_out": w(ks[11], (N_CONV_LAYERS, D_MODEL, D_MODEL), D_MODEL),
        "ret_w_in": w(ks[12], (N_RET_LAYERS, D_MODEL, ret_in_width), D_MODEL),
        "ret_w_out": w(ks[13], (N_RET_LAYERS, RET_HEADS * RET_V_DIM, D_MODEL), RET_HEADS * RET_V_DIM),
        "ple_w_proj": w(ks[14], (DEPTH, D_PLE, D_MODEL), D_PLE),
        "ple_w_gate": w(ks[15], (DEPTH, D_MODEL, D_MODEL), D_MODEL),
    }


def reference(x, p, norm_g, ffn1_w_gate, ffn1_w_up, ffn1_w_down, ffn2_w_gate, ffn2_w_up, ffn2_w_down,
              conv_w_in, conv_w, conv_w_out, ret_w_in, ret_w_out, ple_w_proj, ple_w_gate):
    for i in range(DEPTH):
        g = norm_g[i]
        h = swiglu(rmsnorm(x, g[FFN1_PRE]), ffn1_w_gate[i], ffn1_w_up[i], ffn1_w_down[i])
        x = x + 0.5 * rmsnorm(h, g[FFN1_POST])
        xn = rmsnorm(x, g[MIX_PRE])
        j = i // N_MIXERS
        if i % N_MIXERS == 0:
            h = short_conv_mixer(xn, conv_w_in[j], conv_w[j], conv_w_out[j])
        else:
            h = retention_mixer(xn, ret_w_in[j], ret_w_out[j])
        x = x + rmsnorm(h, g[MIX_POST])
        h = swiglu(rmsnorm(x, g[FFN2_PRE]), ffn2_w_gate[i], ffn2_w_up[i], ffn2_w_down[i])
        x = x + 0.5 * rmsnorm(h, g[FFN2_POST])
        gate = jax.nn.sigmoid(rmsnorm(x, g[PLE_PRE]) @ ple_w_gate[i])
        e = (p[i].astype(x.dtype) @ ple_w_proj[i]) * gate
        x = x + rmsnorm(e, g[PLE_POST])
    return x
```

```python
import functools
import math

import jax
import jax.numpy as jnp
from jax import lax
from jax.experimental import pallas as pl
from jax.experimental.pallas import tpu as pltpu

D_MODEL = 1024
D_FF = 2816
D_PLE = 256
CONV_WIDTH = 3
RET_HEADS = 4
RET_QK_DIM = D_MODEL // RET_HEADS
RET_V_DIM = 2 * RET_QK_DIM
ROPE_BASE = 10000.0
NORM_EPS = 1e-6
N_MIXERS = 2
N_NORMS = 8
FFN1_PRE, FFN1_POST, MIX_PRE, MIX_POST, FFN2_PRE, FFN2_POST, PLE_PRE, PLE_POST = range(N_NORMS)

V7X_VMEM_BYTES = 64 * 1024 * 1024
V7X_SUBLANES = 8

FFN_TILE = 512
CONV_TILE = 512
PLE_TILE = 1024
RET_CHUNK = 256

F32 = jnp.float32
BF16 = jnp.bfloat16


def _rmsnorm(x, g):
    ms = jnp.mean(x * x, axis=-1, keepdims=True)
    return x * lax.rsqrt(ms + NORM_EPS) * g


def _silu(x):
    return x * jax.nn.sigmoid(x)


def _dot(a, b):
    return jnp.dot(a, b, preferred_element_type=F32)


def _vmem_limit(resident_bytes, streamed_bytes, temp_bytes):
    need = resident_bytes + 2 * streamed_bytes + temp_bytes
    return min(int(need * 1.25) + (4 << 20), V7X_VMEM_BYTES - (4 << 20))


def _resident(block_shape, index_map):
    return pl.BlockSpec(block_shape, index_map, pipeline_mode=pl.Buffered(1))


def _ffn_kernel(x_ref, g_ref, wg_ref, wu_ref, wd_ref, o_ref, *, pre, post):
    x = x_ref[...]
    xn = _rmsnorm(x, g_ref[pre:pre + 1, :]).astype(BF16)
    gate = _dot(xn, wg_ref[...])
    up = _dot(xn, wu_ref[...])
    h = (_silu(gate) * up).astype(BF16)
    y = _dot(h, wd_ref[...])
    o_ref[...] = x + 0.5 * _rmsnorm(y, g_ref[post:post + 1, :])


def _ffn(x, norm_g, w_gate, w_up, w_down, layer, pre, post):
    n, d = x.shape
    tm = FFN_TILE
    weights = 3 * d * D_FF * 2
    streamed = 2 * tm * d * 4
    temps = tm * D_FF * (4 + 4 + 2) + 2 * tm * d * 4
    return pl.pallas_call(
        functools.partial(_ffn_kernel, pre=pre, post=post),
        grid=(n // tm,),
        in_specs=[
            pl.BlockSpec((tm, d), lambda i: (i, 0)),
            _resident((None, N_NORMS, d), lambda i: (layer, 0, 0)),
            _resident((None, d, D_FF), lambda i: (layer, 0, 0)),
            _resident((None, d, D_FF), lambda i: (layer, 0, 0)),
            _resident((None, D_FF, d), lambda i: (layer, 0, 0)),
        ],
        out_specs=pl.BlockSpec((tm, d), lambda i: (i, 0)),
        out_shape=jax.ShapeDtypeStruct((n, d), F32),
        compiler_params=pltpu.CompilerParams(
            dimension_semantics=("arbitrary",),
            vmem_limit_bytes=_vmem_limit(weights, streamed, temps)),
        name=f"ffn_l{layer}_n{pre}",
    )(x, norm_g, w_gate, w_up, w_down)


def _conv_kernel(x_ref, g_ref, win_ref, cw_ref, wout_ref, o_ref, tail_ref):
    tm, d = x_ref.shape

    @pl.when(pl.program_id(1) == 0)
    def _():
        tail_ref[...] = jnp.zeros_like(tail_ref)

    x = x_ref[...]
    xn = _rmsnorm(x, g_ref[MIX_PRE:MIX_PRE + 1, :]).astype(BF16)
    proj = _dot(xn, win_ref[...])
    b = proj[:, :d]
    u = proj[:, d:2 * d] * proj[:, 2 * d:]
    tail = tail_ref[...]
    row = lax.broadcasted_iota(jnp.int32, (tm, 1), 0)
    u1 = jnp.where(row == 0, tail[V7X_SUBLANES - 1:, :], pltpu.roll(u, 1, axis=0))
    u2 = pltpu.roll(u, 2, axis=0)
    u2 = jnp.where(row == 0, tail[V7X_SUBLANES - 2:V7X_SUBLANES - 1, :],
                   jnp.where(row == 1, tail[V7X_SUBLANES - 1:, :], u2))
    tail_ref[...] = u[tm - V7X_SUBLANES:, :]
    v = cw_ref[0:1, :] * u2 + cw_ref[1:2, :] * u1 + cw_ref[2:3, :] * u
    y = _dot((b * v).astype(BF16), wout_ref[...])
    o_ref[...] = x + _rmsnorm(y, g_ref[MIX_POST:MIX_POST + 1, :])


def _conv_mixer(x, norm_g, w_in, conv_w, w_out, layer, j):
    bsz, t, d = x.shape
    tm = CONV_TILE
    weights = 4 * d * d * 2
    streamed = 2 * tm * d * 4
    temps = tm * 3 * d * 4 + 6 * tm * d * 4
    return pl.pallas_call(
        _conv_kernel,
        grid=(bsz, t // tm),
        in_specs=[
            pl.BlockSpec((None, tm, d), lambda bi, ti: (bi, ti, 0)),
            _resident((None, N_NORMS, d), lambda bi, ti: (layer, 0, 0)),
            _resident((None, d, 3 * d), lambda bi, ti: (j, 0, 0)),
            _resident((None, CONV_WIDTH, d), lambda bi, ti: (j, 0, 0)),
            _resident((None, d, d), lambda bi, ti: (j, 0, 0)),
        ],
        out_specs=pl.BlockSpec((None, tm, d), lambda bi, ti: (bi, ti, 0)),
        out_shape=jax.ShapeDtypeStruct((bsz, t, d), F32),
        scratch_shapes=[pltpu.VMEM((V7X_SUBLANES, d), F32)],
        compiler_params=pltpu.CompilerParams(
            dimension_semantics=("arbitrary", "arbitrary"),
            vmem_limit_bytes=_vmem_limit(weights, streamed, temps)),
        name=f"conv_l{layer}",
    )(x, norm_g, w_in, conv_w, w_out)


def _rotate(a, cos, sin):
    half = a.shape[-1] // 2
    a1, a2 = a[:, :half], a[:, half:]
    return jnp.concatenate([a1 * cos - a2 * sin, a2 * cos + a1 * sin], axis=-1)


def _ret_kernel(x_ref, g_ref, cos_ref, sin_ref, win_ref, wout_ref, o_ref, state_ref):
    c, d = x_ref.shape
    nh, dk, dv = RET_HEADS, RET_QK_DIM, RET_V_DIM

    @pl.when(pl.program_id(1) == 0)
    def _():
        state_ref[...] = jnp.zeros_like(state_ref)

    x = x_ref[...]
    xn = _rmsnorm(x, g_ref[MIX_PRE:MIX_PRE + 1, :]).astype(BF16)
    proj = _dot(xn, win_ref[...])
    cos = cos_ref[...]
    sin = sin_ref[...]
    ri = lax.broadcasted_iota(jnp.int32, (c, c), 0)
    ci = lax.broadcasted_iota(jnp.int32, (c, c), 1)
    diff = (ri - ci).astype(F32)
    idx = lax.broadcasted_iota(jnp.int32, (c, 1), 0).astype(F32)
    ys = []
    for h in range(nh):
        log_gamma = math.log1p(-(2.0 ** (-5.0 - h)))
        q = _rotate(proj[:, h * dk:(h + 1) * dk], cos, sin)
        k = _rotate(proj[:, nh * dk + h * dk:nh * dk + (h + 1) * dk], cos, sin) * (dk ** -0.5)
        v = proj[:, 2 * nh * dk + h * dv:2 * nh * dk + (h + 1) * dv].astype(BF16)
        gt = proj[:, 2 * nh * dk + nh * dv + h * dv:2 * nh * dk + nh * dv + (h + 1) * dv]
        intra_decay = jnp.where(diff >= 0.0, jnp.exp(log_gamma * jnp.maximum(diff, 0.0)), 0.0)
        q_decay = jnp.exp(log_gamma * (idx + 1.0))
        k_decay = jnp.exp(log_gamma * (c - 1.0 - idx))
        chunk_decay = math.exp(log_gamma * c)
        state = state_ref[h]
        scores = lax.dot_general(q.astype(BF16), k.astype(BF16), (((1,), (1,)), ((), ())),
                                 preferred_element_type=F32) * intra_decay
        inner = _dot(scores.astype(BF16), v)
        cross = _dot((q * q_decay).astype(BF16), state.astype(BF16))
        kd_t = (k * k_decay).T.astype(BF16)
        state_ref[h] = chunk_decay * state + _dot(kd_t, v)
        out = inner + cross
        out = out * lax.rsqrt(jnp.mean(out * out, axis=-1, keepdims=True) + NORM_EPS)
        ys.append((_silu(gt) * out).astype(BF16))
    y = _dot(jnp.concatenate(ys, axis=-1), wout_ref[...])
    o_ref[...] = x + _rmsnorm(y, g_ref[MIX_POST:MIX_POST + 1, :])


def _ret_mixer(x, norm_g, cos, sin, w_in, w_out, layer, j):
    bsz, t, d = x.shape
    c = RET_CHUNK
    width = w_in.shape[-1]
    half = RET_QK_DIM // 2
    weights = (d * width + RET_HEADS * RET_V_DIM * d) * 2
    streamed = 2 * c * d * 4 + 2 * c * half * 4
    temps = c * width * 4 * 2 + RET_HEADS * RET_QK_DIM * RET_V_DIM * 4 + 8 * c * d * 4
    return pl.pallas_call(
        _ret_kernel,
        grid=(bsz, t // c),
        in_specs=[
            pl.BlockSpec((None, c, d), lambda bi, ti: (bi, ti, 0)),
            _resident((None, N_NORMS, d), lambda bi, ti: (layer, 0, 0)),
            pl.BlockSpec((c, half), lambda bi, ti: (ti, 0)),
            pl.BlockSpec((c, half), lambda bi, ti: (ti, 0)),
            _resident((None, d, width), lambda bi, ti: (j, 0, 0)),
            _resident((None, RET_HEADS * RET_V_DIM, d), lambda bi, ti: (j, 0, 0)),
        ],
        out_specs=pl.BlockSpec((None, c, d), lambda bi, ti: (bi, ti, 0)),
        out_shape=jax.ShapeDtypeStruct((bsz, t, d), F32),
        scratch_shapes=[pltpu.VMEM((RET_HEADS, RET_QK_DIM, RET_V_DIM), F32)],
        compiler_params=pltpu.CompilerParams(
            dimension_semantics=("arbitrary", "arbitrary"),
            vmem_limit_bytes=_vmem_limit(weights, streamed, temps)),
        name=f"ret_l{layer}",
    )(x, norm_g, cos, sin, w_in, w_out)


def _ple_kernel(x_ref, p_ref, g_ref, wproj_ref, wgate_ref, o_ref):
    x = x_ref[...]
    xn = _rmsnorm(x, g_ref[PLE_PRE:PLE_PRE + 1, :]).astype(BF16)
    gate = jax.nn.sigmoid(_dot(xn, wgate_ref[...]))
    e = _dot(p_ref[...].astype(BF16), wproj_ref[...]) * gate
    o_ref[...] = x + _rmsnorm(e, g_ref[PLE_POST:PLE_POST + 1, :])


def _ple(x, p, norm_g, w_proj, w_gate, layer):
    n, d = x.shape
    tm = PLE_TILE
    weights = (D_PLE * d + d * d) * 2
    streamed = 2 * tm * d * 4 + tm * D_PLE * 4
    temps = 4 * tm * d * 4
    return pl.pallas_call(
        _ple_kernel,
        grid=(n // tm,),
        in_specs=[
            pl.BlockSpec((tm, d), lambda i: (i, 0)),
            pl.BlockSpec((None, tm, D_PLE), lambda i: (layer, i, 0)),
            _resident((None, N_NORMS, d), lambda i: (layer, 0, 0)),
            _resident((None, D_PLE, d), lambda i: (layer, 0, 0)),
            _resident((None, d, d), lambda i: (layer, 0, 0)),
        ],
        out_specs=pl.BlockSpec((tm, d), lambda i: (i, 0)),
        out_shape=jax.ShapeDtypeStruct((n, d), F32),
        compiler_params=pltpu.CompilerParams(
            dimension_semantics=("arbitrary",),
            vmem_limit_bytes=_vmem_limit(weights, streamed, temps)),
        name=f"ple_l{layer}",
    )(x, p, norm_g, w_proj, w_gate)


def _rope_tables(t):
    half = RET_QK_DIM // 2
    inv_freq = 1.0 / (ROPE_BASE ** jnp.linspace(0.0, 1.0, half, dtype=F32))
    ang = jnp.arange(t).astype(F32)[:, None] * inv_freq[None, :]
    return jnp.cos(ang), jnp.sin(ang)


def kernel(x, p, norm_g, ffn1_w_gate, ffn1_w_up, ffn1_w_down, ffn2_w_gate, ffn2_w_up, ffn2_w_down,
           conv_w_in, conv_w, conv_w_out, ret_w_in, ret_w_out, ple_w_proj, ple_w_gate):
    depth, bsz, t, _ = p.shape
    d = x.shape[-1]
    n = bsz * t
    bf = lambda w: w.astype(BF16)
    ffn1 = (bf(ffn1_w_gate), bf(ffn1_w_up), bf(ffn1_w_down))
    ffn2 = (bf(ffn2_w_gate), bf(ffn2_w_up), bf(ffn2_w_down))
    conv_in, conv_out = bf(conv_w_in), bf(conv_w_out)
    ret_in, ret_out = bf(ret_w_in), bf(ret_w_out)
    ple_proj, ple_gate = bf(ple_w_proj), bf(ple_w_gate)
    p2 = p.reshape(depth, n, D_PLE)
    cos, sin = _rope_tables(t)

    x = x.reshape(n, d)
    for i in range(depth):
        j = i // N_MIXERS
        x = _ffn(x, norm_g, *ffn1, i, FFN1_PRE, FFN1_POST)
        x3 = x.reshape(bsz, t, d)
        if i % N_MIXERS == 0:
            x3 = _conv_mixer(x3, norm_g, conv_in, conv_w, conv_out, i, j)
        else:
            x3 = _ret_mixer(x3, norm_g, cos, sin, ret_in, ret_out, i, j)
        x = x3.reshape(n, d)
        x = _ffn(x, norm_g, *ffn2, i, FFN2_PRE, FFN2_POST)
        x = _ple(x, p2, norm_g, ple_proj, ple_gate, i)
    return x.reshape(bsz, t, d)
```

```python
import functools
import math

import jax
import jax.numpy as jnp
from jax import lax
from jax.experimental import pallas as pl
from jax.experimental.pallas import tpu as pltpu

D_MODEL = 1024
D_FF = 2816
D_PLE = 256
CONV_WIDTH = 3
RET_HEADS = 4
RET_QK_DIM = D_MODEL // RET_HEADS
RET_V_DIM = 2 * RET_QK_DIM
ROPE_BASE = 10000.0
NORM_EPS = 1e-6
N_MIXERS = 2
N_NORMS = 8
FFN1_PRE, FFN1_POST, MIX_PRE, MIX_POST, FFN2_PRE, FFN2_POST, PLE_PRE, PLE_POST = range(N_NORMS)

V7X_VMEM_BYTES = 64 * 1024 * 1024
V7X_SUBLANES = 8
MXU_DEPTH = 256

FFN_TILE = 512
CONV_TILE = 512
RET_CHUNK = 256

F32 = jnp.float32
BF16 = jnp.bfloat16


def _rmsnorm(x, g):
    ms = jnp.mean(x * x, axis=-1, keepdims=True)
    return x * lax.rsqrt(ms + NORM_EPS) * g


def _silu(x):
    return x * jax.nn.sigmoid(x)


def _dot(a, b):
    return jnp.dot(a, b, preferred_element_type=F32)


def _ordered_after(dst, src):
    words = pltpu.bitcast(src, jnp.uint32)
    width = dst.shape[1]
    acc = words[:, :width]
    for k in range(1, src.shape[1] // width):
        acc = acc | words[:, k * width:(k + 1) * width]
    zero = lax.shift_right_logical(lax.shift_right_logical(acc, jnp.uint32(16)), jnp.uint32(16))
    return dst + pltpu.bitcast(zero, dst.dtype)


def _vmem_limit(resident_bytes, streamed_bytes, temp_bytes):
    need = resident_bytes + 2 * streamed_bytes + temp_bytes
    return min(int(need * 1.25) + (4 << 20), V7X_VMEM_BYTES - (4 << 20))


def _resident(block_shape, index_map):
    return pl.BlockSpec(block_shape, index_map, pipeline_mode=pl.Buffered(1))


def _ffn_kernel(*refs, pre, post, n_tiles, with_ple):
    if with_ple:
        (xcur_ref, xprev_ref, g_ref, wg_ref, wu_ref, wd_ref, p_ref, wpp_ref, wpg_ref,
         o_ref, h_ref) = refs
    else:
        xcur_ref, xprev_ref, g_ref, wg_ref, wu_ref, wd_ref, o_ref, h_ref = refs
    i = pl.program_id(0)
    slot = lax.rem(i, 2)
    other = 1 - slot

    def gain(k):
        return g_ref[k:k + 1, :]

    def normalise_cur():
        return _rmsnorm(xcur_ref[...], gain(pre)).astype(BF16)

    def down_prev(xn=None):
        h = h_ref[other]
        if xn is not None:
            k1 = h.shape[1] - MXU_DEPTH
            h = jnp.concatenate([h[:, :k1], _ordered_after(h[:, k1:], xn)], axis=-1)
        return xprev_ref[...] + 0.5 * _rmsnorm(_dot(h, wd_ref[...]), gain(post))

    def embed_prev(x1):
        if with_ple:
            gate = jax.nn.sigmoid(_dot(_rmsnorm(x1, gain(PLE_PRE)).astype(BF16), wpg_ref[...]))
            e = _dot(p_ref[...].astype(BF16), wpp_ref[...]) * gate
            x1 = x1 + _rmsnorm(e, gain(PLE_POST))
        o_ref[...] = x1

    def store_h(gate, up):
        h_ref[slot] = (_silu(gate) * up).astype(BF16)

    @pl.when(i == 0)
    def _():
        xn = normalise_cur()
        store_h(_dot(xn, wg_ref[...]), _dot(xn, wu_ref[...]))

    @pl.when(jnp.logical_and(i > 0, i < n_tiles))
    def _():
        xn = normalise_cur()
        x1 = down_prev(xn)
        gate = _dot(xn, wg_ref[...])
        embed_prev(x1)
        store_h(gate, _dot(xn, wu_ref[...]))

    @pl.when(i == n_tiles)
    def _():
        embed_prev(down_prev())


def _ffn(x, norm_g, w_gate, w_up, w_down, layer, pre, post, ple=None):
    n, d = x.shape
    tm = FFN_TILE
    n_tiles = n // tm
    last = n_tiles - 1
    cur = lambda i: (jnp.minimum(i, last), 0)
    prev = lambda i: (jnp.maximum(i - 1, 0), 0)
    whole = lambda i: (layer, 0, 0)
    weights = 3 * d * D_FF * 2
    streamed = 3 * tm * d * 4
    temps = tm * D_FF * (4 + 4 + 2 * 2) + 3 * tm * d * 4
    in_specs = [
        pl.BlockSpec((tm, d), cur),
        pl.BlockSpec((tm, d), prev),
        _resident((None, N_NORMS, d), whole),
        _resident((None, d, D_FF), whole),
        _resident((None, d, D_FF), whole),
        _resident((None, D_FF, d), whole),
    ]
    operands = [x, x, norm_g, w_gate, w_up, w_down]
    if ple is not None:
        in_specs += [
            pl.BlockSpec((None, tm, D_PLE), lambda i: (layer, jnp.maximum(i - 1, 0), 0)),
            _resident((None, D_PLE, d), whole),
            _resident((None, d, d), whole),
        ]
        operands += list(ple)
        weights += (D_PLE * d + d * d) * 2
        streamed += tm * D_PLE * 4
        temps += 3 * tm * d * 4
    return pl.pallas_call(
        functools.partial(_ffn_kernel, pre=pre, post=post, n_tiles=n_tiles,
                          with_ple=ple is not None),
        grid=(n_tiles + 1,),
        in_specs=in_specs,
        out_specs=pl.BlockSpec((tm, d), prev),
        out_shape=jax.ShapeDtypeStruct((n, d), F32),
        scratch_shapes=[pltpu.VMEM((2, tm, D_FF), BF16)],
        compiler_params=pltpu.CompilerParams(
            dimension_semantics=("arbitrary",),
            vmem_limit_bytes=_vmem_limit(weights, streamed, temps)),
        name=f"ffn_l{layer}_n{pre}",
    )(*operands)


def _conv_kernel(x_ref, g_ref, win_ref, cw_ref, wout_ref, o_ref, tail_ref):
    tm, d = x_ref.shape

    @pl.when(pl.program_id(1) == 0)
    def _():
        tail_ref[...] = jnp.zeros_like(tail_ref)

    x = x_ref[...]
    xn = _rmsnorm(x, g_ref[MIX_PRE:MIX_PRE + 1, :]).astype(BF16)
    proj = _dot(xn, win_ref[...])
    b = proj[:, :d]
    u = proj[:, d:2 * d] * proj[:, 2 * d:]
    tail = tail_ref[...]
    row = lax.broadcasted_iota(jnp.int32, (tm, 1), 0)
    u1 = jnp.where(row == 0, tail[V7X_SUBLANES - 1:, :], pltpu.roll(u, 1, axis=0))
    u2 = pltpu.roll(u, 2, axis=0)
    u2 = jnp.where(row == 0, tail[V7X_SUBLANES - 2:V7X_SUBLANES - 1, :],
                   jnp.where(row == 1, tail[V7X_SUBLANES - 1:, :], u2))
    tail_ref[...] = u[tm - V7X_SUBLANES:, :]
    v = cw_ref[0:1, :] * u2 + cw_ref[1:2, :] * u1 + cw_ref[2:3, :] * u
    y = _dot((b * v).astype(BF16), wout_ref[...])
    o_ref[...] = x + _rmsnorm(y, g_ref[MIX_POST:MIX_POST + 1, :])


def _conv_mixer(x, norm_g, w_in, conv_w, w_out, layer, j):
    bsz, t, d = x.shape
    tm = CONV_TILE
    weights = 4 * d * d * 2
    streamed = 2 * tm * d * 4
    temps = tm * 3 * d * 4 + 6 * tm * d * 4
    return pl.pallas_call(
        _conv_kernel,
        grid=(bsz, t // tm),
        in_specs=[
            pl.BlockSpec((None, tm, d), lambda bi, ti: (bi, ti, 0)),
            _resident((None, N_NORMS, d), lambda bi, ti: (layer, 0, 0)),
            _resident((None, d, 3 * d), lambda bi, ti: (j, 0, 0)),
            _resident((None, CONV_WIDTH, d), lambda bi, ti: (j, 0, 0)),
            _resident((None, d, d), lambda bi, ti: (j, 0, 0)),
        ],
        out_specs=pl.BlockSpec((None, tm, d), lambda bi, ti: (bi, ti, 0)),
        out_shape=jax.ShapeDtypeStruct((bsz, t, d), F32),
        scratch_shapes=[pltpu.VMEM((V7X_SUBLANES, d), F32)],
        compiler_params=pltpu.CompilerParams(
            dimension_semantics=("arbitrary", "arbitrary"),
            vmem_limit_bytes=_vmem_limit(weights, streamed, temps)),
        name=f"conv_l{layer}",
    )(x, norm_g, w_in, conv_w, w_out)


def _rotate(a, cos, sin):
    half = a.shape[-1] // 2
    a1, a2 = a[:, :half], a[:, half:]
    return jnp.concatenate([a1 * cos - a2 * sin, a2 * cos + a1 * sin], axis=-1)


def _ret_kernel(x_ref, g_ref, cos_ref, sin_ref, win_ref, wout_ref, o_ref, state_ref):
    c, d = x_ref.shape
    nh, dk, dv = RET_HEADS, RET_QK_DIM, RET_V_DIM

    @pl.when(pl.program_id(1) == 0)
    def _():
        state_ref[...] = jnp.zeros_like(state_ref)

    x = x_ref[...]
    xn = _rmsnorm(x, g_ref[MIX_PRE:MIX_PRE + 1, :]).astype(BF16)
    proj = _dot(xn, win_ref[...])
    cos = cos_ref[...]
    sin = sin_ref[...]
    ri = lax.broadcasted_iota(jnp.int32, (c, c), 0)
    ci = lax.broadcasted_iota(jnp.int32, (c, c), 1)
    diff = (ri - ci).astype(F32)
    idx = lax.broadcasted_iota(jnp.int32, (c, 1), 0).astype(F32)
    ys = []
    for h in range(nh):
        log_gamma = math.log1p(-(2.0 ** (-5.0 - h)))
        q = _rotate(proj[:, h * dk:(h + 1) * dk], cos, sin)
        k = _rotate(proj[:, nh * dk + h * dk:nh * dk + (h + 1) * dk], cos, sin) * (dk ** -0.5)
        v = proj[:, 2 * nh * dk + h * dv:2 * nh * dk + (h + 1) * dv].astype(BF16)
        gt = proj[:, 2 * nh * dk + nh * dv + h * dv:2 * nh * dk + nh * dv + (h + 1) * dv]
        intra_decay = jnp.where(diff >= 0.0, jnp.exp(log_gamma * jnp.maximum(diff, 0.0)), 0.0)
        q_decay = jnp.exp(log_gamma * (idx + 1.0))
        k_decay = jnp.exp(log_gamma * (c - 1.0 - idx))
        chunk_decay = math.exp(log_gamma * c)
        state = state_ref[h]
        scores = lax.dot_general(q.astype(BF16), k.astype(BF16), (((1,), (1,)), ((), ())),
                                 preferred_element_type=F32) * intra_decay
        inner = _dot(scores.astype(BF16), v)
        cross = _dot((q * q_decay).astype(BF16), state.astype(BF16))
        kd_t = (k * k_decay).T.astype(BF16)
        state_ref[h] = chunk_decay * state + _dot(kd_t, v)
        out = inner + cross
        out = out * lax.rsqrt(jnp.mean(out * out, axis=-1, keepdims=True) + NORM_EPS)
        ys.append((_silu(gt) * out).astype(BF16))
    y = _dot(jnp.concatenate(ys, axis=-1), wout_ref[...])
    o_ref[...] = x + _rmsnorm(y, g_ref[MIX_POST:MIX_POST + 1, :])


def _ret_mixer(x, norm_g, cos, sin, w_in, w_out, layer, j):
    bsz, t, d = x.shape
    c = RET_CHUNK
    width = w_in.shape[-1]
    half = RET_QK_DIM // 2
    weights = (d * width + RET_HEADS * RET_V_DIM * d) * 2
    streamed = 2 * c * d * 4 + 2 * c * half * 4
    temps = c * width * 4 * 2 + RET_HEADS * RET_QK_DIM * RET_V_DIM * 4 + 8 * c * d * 4
    return pl.pallas_call(
        _ret_kernel,
        grid=(bsz, t // c),
        in_specs=[
            pl.BlockSpec((None, c, d), lambda bi, ti: (bi, ti, 0)),
            _resident((None, N_NORMS, d), lambda bi, ti: (layer, 0, 0)),
            pl.BlockSpec((c, half), lambda bi, ti: (ti, 0)),
            pl.BlockSpec((c, half), lambda bi, ti: (ti, 0)),
            _resident((None, d, width), lambda bi, ti: (j, 0, 0)),
            _resident((None, RET_HEADS * RET_V_DIM, d), lambda bi, ti: (j, 0, 0)),
        ],
        out_specs=pl.BlockSpec((None, c, d), lambda bi, ti: (bi, ti, 0)),
        out_shape=jax.ShapeDtypeStruct((bsz, t, d), F32),
        scratch_shapes=[pltpu.VMEM((RET_HEADS, RET_QK_DIM, RET_V_DIM), F32)],
        compiler_params=pltpu.CompilerParams(
            dimension_semantics=("arbitrary", "arbitrary"),
            vmem_limit_bytes=_vmem_limit(weights, streamed, temps)),
        name=f"ret_l{layer}",
    )(x, norm_g, cos, sin, w_in, w_out)


def _rope_tables(t):
    half = RET_QK_DIM // 2
    inv_freq = 1.0 / (ROPE_BASE ** jnp.linspace(0.0, 1.0, half, dtype=F32))
    ang = jnp.arange(t).astype(F32)[:, None] * inv_freq[None, :]
    return jnp.cos(ang), jnp.sin(ang)


def kernel(x, p, norm_g, ffn1_w_gate, ffn1_w_up, ffn1_w_down, ffn2_w_gate, ffn2_w_up, ffn2_w_down,
           conv_w_in, conv_w, conv_w_out, ret_w_in, ret_w_out, ple_w_proj, ple_w_gate):
    depth, bsz, t, _ = p.shape
    d = x.shape[-1]
    n = bsz * t
    bf = lambda w: w.astype(BF16)
    ffn1 = (bf(ffn1_w_gate), bf(ffn1_w_up), bf(ffn1_w_down))
    ffn2 = (bf(ffn2_w_gate), bf(ffn2_w_up), bf(ffn2_w_down))
    conv_in, conv_out = bf(conv_w_in), bf(conv_w_out)
    ret_in, ret_out = bf(ret_w_in), bf(ret_w_out)
    ple_proj, ple_gate = bf(ple_w_proj), bf(ple_w_gate)
    p2 = p.reshape(depth, n, D_PLE)
    cos, sin = _rope_tables(t)

    x = x.reshape(n, d)
    for i in range(depth):
        j = i // N_MIXERS
        x = _ffn(x, norm_g, *ffn1, i, FFN1_PRE, FFN1_POST)
        x3 = x.reshape(bsz, t, d)
        if i % N_MIXERS == 0:
            x3 = _conv_mixer(x3, norm_g, conv_in, conv_w, conv_out, i, j)
        else:
            x3 = _ret_mixer(x3, norm_g, cos, sin, ret_in, ret_out, i, j)
        x = x3.reshape(n, d)
        x = _ffn(x, norm_g, *ffn2, i, FFN2_PRE, FFN2_POST, ple=(p2, ple_proj, ple_gate))
    return x.reshape(bsz, t, d)
```

```python
import functools
import math

import jax
import jax.numpy as jnp
from jax import lax
from jax.experimental import pallas as pl
from jax.experimental.pallas import tpu as pltpu

D_MODEL = 1024
D_FF = 2816
D_PLE = 256
CONV_WIDTH = 3
RET_HEADS = 4
RET_QK_DIM = D_MODEL // RET_HEADS
RET_V_DIM = 2 * RET_QK_DIM
ROPE_BASE = 10000.0
NORM_EPS = 1e-6
N_MIXERS = 2
N_NORMS = 8
FFN1_PRE, FFN1_POST, MIX_PRE, MIX_POST, FFN2_PRE, FFN2_POST, PLE_PRE, PLE_POST = range(N_NORMS)

V7X_VMEM_BYTES = 64 * 1024 * 1024
V7X_SUBLANES = 8
MXU_DEPTH = 256

FFN_TILE = 512
CONV_TILE = 512
RET_CHUNK = 256

F32 = jnp.float32
BF16 = jnp.bfloat16


def _rmsnorm(x, g):
    ms = jnp.mean(x * x, axis=-1, keepdims=True)
    return x * lax.rsqrt(ms + NORM_EPS) * g


def _silu(x):
    return x * jax.nn.sigmoid(x)


def _dot(a, b):
    return jnp.dot(a, b, preferred_element_type=F32)


def _ordered_after(dst, src):
    words = pltpu.bitcast(src, jnp.uint32)
    width = dst.shape[1]
    acc = words[:, :width]
    for k in range(1, src.shape[1] // width):
        acc = acc | words[:, k * width:(k + 1) * width]
    zero = lax.shift_right_logical(lax.shift_right_logical(acc, jnp.uint32(16)), jnp.uint32(16))
    return dst + pltpu.bitcast(zero, dst.dtype)


def _anchored(lhs, xn):
    k1 = lhs.shape[1] - MXU_DEPTH
    return jnp.concatenate([lhs[:, :k1], _ordered_after(lhs[:, k1:], xn)], axis=-1)


def _vmem_limit(resident_bytes, streamed_bytes, temp_bytes):
    need = resident_bytes + 2 * streamed_bytes + temp_bytes
    return min(int(need * 1.25) + (4 << 20), V7X_VMEM_BYTES - (4 << 20))


def _resident(block_shape, index_map):
    return pl.BlockSpec(block_shape, index_map, pipeline_mode=pl.Buffered(1))


def _ffn_kernel(*refs, pre, post, n_tiles, with_ple):
    if with_ple:
        (x_ref, g_ref, wg_ref, wu_ref, wd_ref, p_ref, wpp_ref, wpg_ref,
         o_ref, h_ref, xkeep_ref) = refs
    else:
        x_ref, g_ref, wg_ref, wu_ref, wd_ref, o_ref, h_ref, xkeep_ref = refs
    i = pl.program_id(0)
    slot = lax.rem(i, 2)
    other = 1 - slot

    def gain(k):
        return g_ref[k:k + 1, :]

    def down_prev(xn=None):
        h = h_ref[other]
        if xn is not None:
            h = _anchored(h, xn)
        return xkeep_ref[other] + 0.5 * _rmsnorm(_dot(h, wd_ref[...]), gain(post))

    def embed_prev(x1):
        if with_ple:
            gate = jax.nn.sigmoid(_dot(_rmsnorm(x1, gain(PLE_PRE)).astype(BF16), wpg_ref[...]))
            e = _dot(p_ref[...].astype(BF16), wpp_ref[...]) * gate
            x1 = x1 + _rmsnorm(e, gain(PLE_POST))
        o_ref[...] = x1

    @pl.when(i == 0)
    def _():
        h_ref[1] = jnp.zeros(h_ref.shape[1:], BF16)
        xkeep_ref[1] = jnp.zeros(xkeep_ref.shape[1:], F32)

    @pl.when(i < n_tiles)
    def _():
        x = x_ref[...]
        xn = _rmsnorm(x, gain(pre)).astype(BF16)
        xkeep_ref[slot] = x
        x1 = down_prev(xn)
        gate = _dot(xn, wg_ref[...])
        embed_prev(x1)
        h_ref[slot] = (_silu(gate) * _dot(xn, wu_ref[...])).astype(BF16)

    @pl.when(i == n_tiles)
    def _():
        embed_prev(down_prev())


def _ffn(x, norm_g, w_gate, w_up, w_down, layer, pre, post, ple=None):
    n, d = x.shape
    tm = FFN_TILE
    n_tiles = n // tm
    last = n_tiles - 1
    cur = lambda i: (jnp.minimum(i, last), 0)
    prev = lambda i: (jnp.maximum(i - 1, 0), 0)
    whole = lambda i: (layer, 0, 0)
    weights = 3 * d * D_FF * 2
    streamed = 2 * tm * d * 4
    temps = tm * D_FF * (4 + 4 + 2 * 2) + 5 * tm * d * 4
    in_specs = [
        pl.BlockSpec((tm, d), cur),
        _resident((None, N_NORMS, d), whole),
        _resident((None, d, D_FF), whole),
        _resident((None, d, D_FF), whole),
        _resident((None, D_FF, d), whole),
    ]
    operands = [x, norm_g, w_gate, w_up, w_down]
    if ple is not None:
        in_specs += [
            pl.BlockSpec((None, tm, D_PLE), lambda i: (layer, jnp.maximum(i - 1, 0), 0)),
            _resident((None, D_PLE, d), whole),
            _resident((None, d, d), whole),
        ]
        operands += list(ple)
        weights += (D_PLE * d + d * d) * 2
        streamed += tm * D_PLE * 4
        temps += 3 * tm * d * 4
    return pl.pallas_call(
        functools.partial(_ffn_kernel, pre=pre, post=post, n_tiles=n_tiles,
                          with_ple=ple is not None),
        grid=(n_tiles + 1,),
        in_specs=in_specs,
        out_specs=pl.BlockSpec((tm, d), prev),
        out_shape=jax.ShapeDtypeStruct((n, d), F32),
        scratch_shapes=[pltpu.VMEM((2, tm, D_FF), BF16), pltpu.VMEM((2, tm, d), F32)],
        compiler_params=pltpu.CompilerParams(
            dimension_semantics=("arbitrary",),
            vmem_limit_bytes=_vmem_limit(weights, streamed, temps)),
        name=f"ffn_l{layer}_n{pre}",
    )(*operands)


def _mixer_call(body, x, streamed_specs, streamed_args, resident_specs, resident_args,
                scratch_shapes, tile, vmem_bytes, name):
    n, d = x.shape
    n_tiles = n // tile
    return pl.pallas_call(
        functools.partial(body, n_tiles=n_tiles),
        grid=(n_tiles + 1,),
        in_specs=[pl.BlockSpec((tile, d), lambda s: (jnp.minimum(s, n_tiles - 1), 0))]
        + streamed_specs(n_tiles) + resident_specs,
        out_specs=pl.BlockSpec((tile, d), lambda s: (jnp.maximum(s - 1, 0), 0)),
        out_shape=jax.ShapeDtypeStruct((n, d), F32),
        scratch_shapes=[pltpu.VMEM((2, tile, d), F32)] + scratch_shapes,
        compiler_params=pltpu.CompilerParams(
            dimension_semantics=("arbitrary",), vmem_limit_bytes=vmem_bytes),
        name=name,
    )(x, *streamed_args, *resident_args)


def _conv_kernel(x_ref, g_ref, win_ref, cw_ref, wout_ref, o_ref, xkeep_ref, bv_ref, tail_ref,
                 *, n_tiles, tiles_per_seq):
    tm, d = x_ref.shape
    s = pl.program_id(0)
    slot = lax.rem(s, 2)
    other = 1 - slot

    def close_prev(xn=None):
        bv = bv_ref[other]
        if xn is not None:
            bv = _anchored(bv, xn)
        y = _dot(bv, wout_ref[...])
        o_ref[...] = xkeep_ref[other] + _rmsnorm(y, g_ref[MIX_POST:MIX_POST + 1, :])

    @pl.when(s == 0)
    def _():
        bv_ref[1] = jnp.zeros(bv_ref.shape[1:], BF16)
        xkeep_ref[1] = jnp.zeros(xkeep_ref.shape[1:], F32)

    @pl.when(lax.rem(s, tiles_per_seq) == 0)
    def _():
        tail_ref[...] = jnp.zeros_like(tail_ref)

    @pl.when(s < n_tiles)
    def _():
        x = x_ref[...]
        xn = _rmsnorm(x, g_ref[MIX_PRE:MIX_PRE + 1, :]).astype(BF16)
        xkeep_ref[slot] = x
        close_prev(xn)
        tail = tail_ref[...]
        row = lax.broadcasted_iota(jnp.int32, (tm, 1), 0)
        for a in range(0, d, MXU_DEPTH):
            cols = slice(a, a + MXU_DEPTH)
            b = _dot(xn, win_ref[:, a:a + MXU_DEPTH])
            u = (_dot(xn, win_ref[:, d + a:d + a + MXU_DEPTH])
                 * _dot(xn, win_ref[:, 2 * d + a:2 * d + a + MXU_DEPTH]))
            last1 = tail[V7X_SUBLANES - 1:, cols]
            last2 = tail[V7X_SUBLANES - 2:V7X_SUBLANES - 1, cols]
            u1 = jnp.where(row == 0, last1, pltpu.roll(u, 1, axis=0))
            u2 = jnp.where(row == 0, last2, jnp.where(row == 1, last1, pltpu.roll(u, 2, axis=0)))
            tail_ref[:, cols] = u[tm - V7X_SUBLANES:, :]
            v = cw_ref[0:1, cols] * u2 + cw_ref[1:2, cols] * u1 + cw_ref[2:3, cols] * u
            bv_ref[slot, :, cols] = (b * v).astype(BF16)

    @pl.when(s == n_tiles)
    def _():
        close_prev()


def _conv_mixer(x, norm_g, w_in, conv_w, w_out, layer, j, seq_len):
    n, d = x.shape
    tm = CONV_TILE
    weights = 4 * d * d * 2
    streamed = 2 * tm * d * 4
    temps = 2 * tm * d * (4 + 2) + 10 * tm * MXU_DEPTH * 4 + 2 * tm * d * 4
    whole = lambda s: (layer, 0, 0)
    wj = lambda s: (j, 0, 0)
    return _mixer_call(
        functools.partial(_conv_kernel, tiles_per_seq=seq_len // tm), x,
        lambda n_tiles: [], [],
        [_resident((None, N_NORMS, d), whole), _resident((None, d, 3 * d), wj),
         _resident((None, CONV_WIDTH, d), wj), _resident((None, d, d), wj)],
        [norm_g, w_in, conv_w, w_out],
        [pltpu.VMEM((2, tm, d), BF16), pltpu.VMEM((V7X_SUBLANES, d), F32)],
        tm, _vmem_limit(weights, streamed, temps), f"conv_l{layer}")


def _rotate(a, cos, sin):
    half = a.shape[-1] // 2
    a1, a2 = a[:, :half], a[:, half:]
    return jnp.concatenate([a1 * cos - a2 * sin, a2 * cos + a1 * sin], axis=-1)


def _ret_kernel(x_ref, cos_ref, sin_ref, g_ref, win_ref, wout_ref, o_ref, xkeep_ref, y_ref,
                state_ref, intra_ref, qdec_ref, kdec_ref, *, n_tiles, tiles_per_seq):
    c, d = x_ref.shape
    nh, dk, dv = RET_HEADS, RET_QK_DIM, RET_V_DIM
    s = pl.program_id(0)
    slot = lax.rem(s, 2)
    other = 1 - slot

    def close_prev(xn=None):
        y = y_ref[other]
        if xn is not None:
            y = _anchored(y, xn)
        o_ref[...] = xkeep_ref[other] + _rmsnorm(_dot(y, wout_ref[...]),
                                                 g_ref[MIX_POST:MIX_POST + 1, :])

    @pl.when(s == 0)
    def _():
        y_ref[1] = jnp.zeros(y_ref.shape[1:], BF16)
        xkeep_ref[1] = jnp.zeros(xkeep_ref.shape[1:], F32)
        k_scale = dk ** -0.5
        ri = lax.broadcasted_iota(jnp.int32, (c, c), 0)
        ci = lax.broadcasted_iota(jnp.int32, (c, c), 1)
        diff = (ri - ci).astype(F32)
        idx = lax.broadcasted_iota(jnp.int32, (c, dk), 0).astype(F32)
        for h in range(nh):
            log_gamma = math.log1p(-(2.0 ** (-5.0 - h)))
            intra_ref[h] = jnp.where(
                diff >= 0.0, jnp.exp(log_gamma * jnp.maximum(diff, 0.0)), 0.0) * k_scale
            qdec_ref[h] = jnp.exp(log_gamma * (idx + 1.0))
            kdec_ref[h] = jnp.exp(log_gamma * (c - 1.0 - idx)) * k_scale

    @pl.when(lax.rem(s, tiles_per_seq) == 0)
    def _():
        state_ref[...] = jnp.zeros_like(state_ref)

    @pl.when(s < n_tiles)
    def _():
        x = x_ref[...]
        xn = _rmsnorm(x, g_ref[MIX_PRE:MIX_PRE + 1, :]).astype(BF16)
        xkeep_ref[slot] = x
        close_prev(xn)
        cos = cos_ref[...]
        sin = sin_ref[...]
        k_off, v_off, g_off = nh * dk, 2 * nh * dk, 2 * nh * dk + nh * dv

        def project(h):
            return (_dot(xn, win_ref[:, h * dk:(h + 1) * dk]),
                    _dot(xn, win_ref[:, k_off + h * dk:k_off + (h + 1) * dk]),
                    _dot(xn, win_ref[:, v_off + h * dv:v_off + (h + 1) * dv]),
                    _dot(xn, win_ref[:, g_off + h * dv:g_off + (h + 1) * dv]))

        def retain(h, q, k, v, gt):
            chunk_decay = math.exp(math.log1p(-(2.0 ** (-5.0 - h))) * c)
            q = _rotate(q, cos, sin)
            k = _rotate(k, cos, sin)
            v = v.astype(BF16)
            state = state_ref[h]
            scores = lax.dot_general(q.astype(BF16), k.astype(BF16), (((1,), (1,)), ((), ())),
                                     preferred_element_type=F32) * intra_ref[h]
            inner = _dot(scores.astype(BF16), v)
            cross = _dot((q * qdec_ref[h]).astype(BF16), state.astype(BF16))
            kd_t = (k * kdec_ref[h]).T.astype(BF16)
            state_ref[h] = chunk_decay * state + _dot(kd_t, v)
            out = inner + cross
            out = out * lax.rsqrt(jnp.mean(out * out, axis=-1, keepdims=True) + NORM_EPS)
            y_ref[slot, :, h * dv:(h + 1) * dv] = (_silu(gt) * out).astype(BF16)

        pending = project(0)
        for h in range(nh):
            ready, pending = pending, (project(h + 1) if h + 1 < nh else None)
            retain(h, *ready)

    @pl.when(s == n_tiles)
    def _():
        close_prev()


def _ret_mixer(x, norm_g, cos, sin, w_in, w_out, layer, j, seq_len):
    n, d = x.shape
    c = RET_CHUNK
    width = w_in.shape[-1]
    half = RET_QK_DIM // 2
    tiles_per_seq = seq_len // c
    weights = (d * width + RET_HEADS * RET_V_DIM * d) * 2
    streamed = 2 * c * d * 4 + 2 * c * half * 4
    temps = (c * width * 4 + RET_HEADS * RET_QK_DIM * RET_V_DIM * 4 + 2 * c * d * 4
             + 2 * c * RET_HEADS * RET_V_DIM * 2 + 8 * c * d * 4
             + RET_HEADS * c * (c + 2 * RET_QK_DIM) * 4)
    whole = lambda s: (layer, 0, 0)
    wj = lambda s: (j, 0, 0)

    def table_specs(n_tiles):
        pos = lambda s: (lax.rem(jnp.minimum(s, n_tiles - 1), tiles_per_seq), 0)
        return [pl.BlockSpec((c, half), pos), pl.BlockSpec((c, half), pos)]

    return _mixer_call(
        functools.partial(_ret_kernel, tiles_per_seq=tiles_per_seq), x,
        table_specs, [cos, sin],
        [_resident((None, N_NORMS, d), whole), _resident((None, d, width), wj),
         _resident((None, RET_HEADS * RET_V_DIM, d), wj)],
        [norm_g, w_in, w_out],
        [pltpu.VMEM((2, c, RET_HEADS * RET_V_DIM), BF16),
         pltpu.VMEM((RET_HEADS, RET_QK_DIM, RET_V_DIM), F32),
         pltpu.VMEM((RET_HEADS, c, c), F32),
         pltpu.VMEM((RET_HEADS, c, RET_QK_DIM), F32),
         pltpu.VMEM((RET_HEADS, c, RET_QK_DIM), F32)],
        c, _vmem_limit(weights, streamed, temps), f"ret_l{layer}")


def _rope_tables(t):
    half = RET_QK_DIM // 2
    inv_freq = 1.0 / (ROPE_BASE ** jnp.linspace(0.0, 1.0, half, dtype=F32))
    ang = jnp.arange(t).astype(F32)[:, None] * inv_freq[None, :]
    return jnp.cos(ang), jnp.sin(ang)


def kernel(x, p, norm_g, ffn1_w_gate, ffn1_w_up, ffn1_w_down, ffn2_w_gate, ffn2_w_up, ffn2_w_down,
           conv_w_in, conv_w, conv_w_out, ret_w_in, ret_w_out, ple_w_proj, ple_w_gate):
    depth, bsz, t, _ = p.shape
    d = x.shape[-1]
    n = bsz * t
    bf = lambda w: w.astype(BF16)
    ffn1 = (bf(ffn1_w_gate), bf(ffn1_w_up), bf(ffn1_w_down))
    ffn2 = (bf(ffn2_w_gate), bf(ffn2_w_up), bf(ffn2_w_down))
    conv_in, conv_out = bf(conv_w_in), bf(conv_w_out)
    ret_in, ret_out = bf(ret_w_in), bf(ret_w_out)
    ple_proj, ple_gate = bf(ple_w_proj), bf(ple_w_gate)
    p2 = p.reshape(depth, n, D_PLE)
    cos, sin = _rope_tables(t)

    x = x.reshape(n, d)
    for i in range(depth):
        j = i // N_MIXERS
        x = _ffn(x, norm_g, *ffn1, i, FFN1_PRE, FFN1_POST)
        if i % N_MIXERS == 0:
            x = _conv_mixer(x, norm_g, conv_in, conv_w, conv_out, i, j, t)
        else:
            x = _ret_mixer(x, norm_g, cos, sin, ret_in, ret_out, i, j, t)
        x = _ffn(x, norm_g, *ffn2, i, FFN2_PRE, FFN2_POST, ple=(p2, ple_proj, ple_gate))
    return x.reshape(bsz, t, d)
```

```python
import functools
import math

import jax
import jax.numpy as jnp
from jax import lax
from jax.experimental import pallas as pl
from jax.experimental.pallas import tpu as pltpu

D_MODEL = 1024
D_FF = 2816
D_PLE = 256
CONV_WIDTH = 3
RET_HEADS = 4
RET_QK_DIM = D_MODEL // RET_HEADS
RET_V_DIM = 2 * RET_QK_DIM
ROPE_BASE = 10000.0
NORM_EPS = 1e-6
N_MIXERS = 2
N_NORMS = 8
FFN1_PRE, FFN1_POST, MIX_PRE, MIX_POST, FFN2_PRE, FFN2_POST, PLE_PRE, PLE_POST = range(N_NORMS)

V7X_VMEM_BYTES = 64 * 1024 * 1024
V7X_SUBLANES = 8
MXU_DEPTH = 256

FFN_TILE = 1024
FFN_SUBTILE_ROWS = (256, 256, 256, 256)
PLE_ROWS = 256
FFN_COL_CHUNK = 1024
CONV_TILE = 2048
CONV_SUBTILES = 4
RET_CHUNK = 256
RET_TILE = 512

F32 = jnp.float32
BF16 = jnp.bfloat16


def _rmsnorm(x, g):
    ms = jnp.mean(x * x, axis=-1, keepdims=True)
    return x * lax.rsqrt(ms + NORM_EPS) * g


def _silu(x):
    return x * jax.nn.sigmoid(x)


def _dot(a, b):
    return jnp.dot(a, b, preferred_element_type=F32)


def _vmem_limit(resident_bytes, streamed_bytes, temp_bytes):
    need = resident_bytes + 2 * streamed_bytes + temp_bytes
    return min(int(need * 1.25) + (4 << 20), V7X_VMEM_BYTES - (4 << 20))


def _resident(block_shape, index_map):
    return pl.BlockSpec(block_shape, index_map, pipeline_mode=pl.Buffered(1))


def _ffn_kernel(*refs, pre, post, with_ple):
    if with_ple:
        x_ref, g_ref, wg_ref, wu_ref, wd_ref, p_ref, wpp_ref, wpg_ref, o_ref = refs
    else:
        x_ref, g_ref, wg_ref, wu_ref, wd_ref, o_ref = refs
    def gain(k):
        return g_ref[k:k + 1, :]

    def half_step(lo, hi):
        x = x_ref[lo:hi, :]
        xn = _rmsnorm(x, gain(pre)).astype(BF16)
        y = None
        for a in range(0, D_FF, FFN_COL_CHUNK):
            b = min(a + FFN_COL_CHUNK, D_FF)
            h = (_silu(_dot(xn, wg_ref[:, a:b])) * _dot(xn, wu_ref[:, a:b])).astype(BF16)
            part = _dot(h, wd_ref[a:b, :])
            y = part if y is None else y + part
        return x + 0.5 * _rmsnorm(y, gain(post))

    def embed(lo, hi, x1):
        p = p_ref[lo:hi, :].astype(BF16)
        gate = jax.nn.sigmoid(_dot(_rmsnorm(x1, gain(PLE_PRE)).astype(BF16), wpg_ref[...]))
        return x1 + _rmsnorm(_dot(p, wpp_ref[...]) * gate, gain(PLE_POST))

    bounds = [sum(FFN_SUBTILE_ROWS[:k]) for k in range(len(FFN_SUBTILE_ROWS) + 1)]
    spans = list(zip(bounds[:-1], bounds[1:]))
    x1 = [half_step(lo, hi) for lo, hi in spans]
    for (lo, hi), v in zip(spans, x1):
        if not with_ple:
            o_ref[lo:hi, :] = v
            continue
        for a in range(lo, hi, PLE_ROWS):
            o_ref[a:a + PLE_ROWS, :] = embed(a, a + PLE_ROWS, v[a - lo:a - lo + PLE_ROWS, :])


def _ffn(x, norm_g, w_gate, w_up, w_down, layer, pre, post, ple=None):
    n, d = x.shape
    tm = FFN_TILE
    assert sum(FFN_SUBTILE_ROWS) == tm and all(r % PLE_ROWS == 0 for r in FFN_SUBTILE_ROWS)
    rows = lambda i: (i, 0)
    whole = lambda i: (layer, 0, 0)
    weights = 3 * d * D_FF * 2
    streamed = 2 * tm * d * 4
    temps = tm * (FFN_COL_CHUNK * (4 + 4 + 2) + 3 * d * 4)
    in_specs = [
        pl.BlockSpec((tm, d), rows),
        _resident((None, N_NORMS, d), whole),
        _resident((None, d, D_FF), whole),
        _resident((None, d, D_FF), whole),
        _resident((None, D_FF, d), whole),
    ]
    operands = [x, norm_g, w_gate, w_up, w_down]
    if ple is not None:
        in_specs += [
            pl.BlockSpec((None, tm, D_PLE), lambda i: (layer, i, 0)),
            _resident((None, D_PLE, d), whole),
            _resident((None, d, d), whole),
        ]
        operands += list(ple)
        weights += (D_PLE * d + d * d) * 2
        streamed += tm * D_PLE * 4
        temps += tm * 3 * d * 4
    return pl.pallas_call(
        functools.partial(_ffn_kernel, pre=pre, post=post, with_ple=ple is not None),
        grid=(n // tm,),
        in_specs=in_specs,
        out_specs=pl.BlockSpec((tm, d), rows),
        out_shape=jax.ShapeDtypeStruct((n, d), F32),
        compiler_params=pltpu.CompilerParams(
            dimension_semantics=("arbitrary",),
            vmem_limit_bytes=_vmem_limit(weights, streamed, temps)),
        name=f"ffn_l{layer}_n{pre}",
    )(*operands)


def _conv_kernel(x_ref, g_ref, win_ref, cw_ref, wout_ref, o_ref, tail_ref, *, tiles_per_seq):
    tm, d = x_ref.shape
    sub = tm // CONV_SUBTILES
    starts = range(0, d, MXU_DEPTH)

    @pl.when(lax.rem(pl.program_id(0), tiles_per_seq) == 0)
    def _():
        tail_ref[...] = jnp.zeros_like(tail_ref)

    row = lax.broadcasted_iota(jnp.int32, (sub, 1), 0)
    tails = [tail_ref[:, a:a + MXU_DEPTH] for a in starts]

    def close(r, x, bv):
        y = _dot(jnp.concatenate(bv, axis=-1), wout_ref[...])
        o_ref[r * sub:(r + 1) * sub, :] = x + _rmsnorm(y, g_ref[MIX_POST:MIX_POST + 1, :])

    pending = None
    for r in range(CONV_SUBTILES):
        x = x_ref[r * sub:(r + 1) * sub, :]
        xn = _rmsnorm(x, g_ref[MIX_PRE:MIX_PRE + 1, :]).astype(BF16)
        bv = []
        for gi, a in enumerate(starts):
            if gi == 1 and pending is not None:
                close(*pending)
            cols = slice(a, a + MXU_DEPTH)
            b = _dot(xn, win_ref[:, a:a + MXU_DEPTH])
            u = (_dot(xn, win_ref[:, d + a:d + a + MXU_DEPTH])
                 * _dot(xn, win_ref[:, 2 * d + a:2 * d + a + MXU_DEPTH]))
            last1 = tails[gi][V7X_SUBLANES - 1:, :]
            last2 = tails[gi][V7X_SUBLANES - 2:V7X_SUBLANES - 1, :]
            u1 = jnp.where(row == 0, last1, pltpu.roll(u, 1, axis=0))
            u2 = jnp.where(row == 0, last2, jnp.where(row == 1, last1, pltpu.roll(u, 2, axis=0)))
            tails[gi] = u[sub - V7X_SUBLANES:, :]
            v = cw_ref[0:1, cols] * u2 + cw_ref[1:2, cols] * u1 + cw_ref[2:3, cols] * u
            bv.append((b * v).astype(BF16))
        pending = (r, x, bv)
    close(*pending)
    for gi, a in enumerate(starts):
        tail_ref[:, a:a + MXU_DEPTH] = tails[gi]


def _conv_mixer(x, norm_g, w_in, conv_w, w_out, layer, j, seq_len):
    n, d = x.shape
    tm = CONV_TILE
    rows = lambda s: (s, 0)
    whole = lambda s: (layer, 0, 0)
    wj = lambda s: (j, 0, 0)
    weights = 4 * d * d * 2
    streamed = 2 * tm * d * 4
    temps = tm * d * (4 + 2) + 12 * tm * MXU_DEPTH * 4 + 2 * tm * d * 4
    return pl.pallas_call(
        functools.partial(_conv_kernel, tiles_per_seq=seq_len // tm),
        grid=(n // tm,),
        in_specs=[
            pl.BlockSpec((tm, d), rows),
            _resident((None, N_NORMS, d), whole),
            _resident((None, d, 3 * d), wj),
            _resident((None, CONV_WIDTH, d), wj),
            _resident((None, d, d), wj),
        ],
        out_specs=pl.BlockSpec((tm, d), rows),
        out_shape=jax.ShapeDtypeStruct((n, d), F32),
        scratch_shapes=[pltpu.VMEM((V7X_SUBLANES, d), F32)],
        compiler_params=pltpu.CompilerParams(
            dimension_semantics=("arbitrary",),
            vmem_limit_bytes=_vmem_limit(weights, streamed, temps)),
        name=f"conv_l{layer}",
    )(x, norm_g, w_in, conv_w, w_out)


def _rotate(a, cos, sin):
    half = a.shape[-1] // 2
    a1, a2 = a[:, :half], a[:, half:]
    return jnp.concatenate([a1 * cos - a2 * sin, a2 * cos + a1 * sin], axis=-1)


def _ret_kernel(x_ref, cos_ref, sin_ref, g_ref, win_ref, wout_ref, o_ref,
                state_ref, intra_ref, qdec_ref, kdec_ref, *, tiles_per_seq):
    c = RET_CHUNK
    nh, dk, dv = RET_HEADS, RET_QK_DIM, RET_V_DIM
    n_chunks = x_ref.shape[0] // c
    s = pl.program_id(0)

    @pl.when(s == 0)
    def _():
        k_scale = dk ** -0.5
        ri = lax.broadcasted_iota(jnp.int32, (c, c), 0)
        ci = lax.broadcasted_iota(jnp.int32, (c, c), 1)
        diff = (ri - ci).astype(F32)
        idx = lax.broadcasted_iota(jnp.int32, (c, dk), 0).astype(F32)
        for h in range(nh):
            log_gamma = math.log1p(-(2.0 ** (-5.0 - h)))
            intra_ref[h] = jnp.where(
                diff >= 0.0, jnp.exp(log_gamma * jnp.maximum(diff, 0.0)), 0.0) * k_scale
            qdec_ref[h] = jnp.exp(log_gamma * (idx + 1.0))
            kdec_ref[h] = jnp.exp(log_gamma * (c - 1.0 - idx)) * k_scale

    @pl.when(lax.rem(s, tiles_per_seq) == 0)
    def _():
        state_ref[...] = jnp.zeros_like(state_ref)

    k_off, v_off, g_off = nh * dk, 2 * nh * dk, 2 * nh * dk + nh * dv

    def rows(r):
        return slice(r * c, (r + 1) * c)

    def project(xn, h):
        return (_dot(xn, win_ref[:, h * dk:(h + 1) * dk]),
                _dot(xn, win_ref[:, k_off + h * dk:k_off + (h + 1) * dk]),
                _dot(xn, win_ref[:, v_off + h * dv:v_off + (h + 1) * dv]),
                _dot(xn, win_ref[:, g_off + h * dv:g_off + (h + 1) * dv]))

    def retain(r, h, q, k, v, gt):
        chunk_decay = math.exp(math.log1p(-(2.0 ** (-5.0 - h))) * c)
        cos = cos_ref[rows(r), :]
        sin = sin_ref[rows(r), :]
        q = _rotate(q, cos, sin)
        k = _rotate(k, cos, sin)
        v = v.astype(BF16)
        state = state_ref[h]
        scores = lax.dot_general(q.astype(BF16), k.astype(BF16), (((1,), (1,)), ((), ())),
                                 preferred_element_type=F32) * intra_ref[h]
        inner = _dot(scores.astype(BF16), v)
        cross = _dot((q * qdec_ref[h]).astype(BF16), state.astype(BF16))
        kd_t = (k * kdec_ref[h]).T.astype(BF16)
        state_ref[h] = chunk_decay * state + _dot(kd_t, v)
        out = inner + cross
        out = out * lax.rsqrt(jnp.mean(out * out, axis=-1, keepdims=True) + NORM_EPS)
        return (_silu(gt) * out).astype(BF16)

    def close(r, x, ys):
        y = _dot(jnp.concatenate(ys, axis=-1), wout_ref[...])
        o_ref[rows(r), :] = x + _rmsnorm(y, g_ref[MIX_POST:MIX_POST + 1, :])

    deferred = []
    for r in range(n_chunks):
        x = x_ref[rows(r), :]
        xn = _rmsnorm(x, g_ref[MIX_PRE:MIX_PRE + 1, :]).astype(BF16)
        ys = [None] * nh
        proj = {0: project(xn, 0)}

        def run_retain(h, r=r, ys=ys, proj=proj):
            ys[h] = retain(r, h, *proj.pop(h))

        for h in range(nh):
            if h + 1 < nh:
                proj[h + 1] = project(xn, h + 1)
            if deferred:
                deferred.pop(0)()
            if h + 1 < nh:
                run_retain(h)
        while deferred:
            deferred.pop(0)()
        deferred = [functools.partial(run_retain, nh - 1),
                    functools.partial(close, r, x, ys)]
    for f in deferred:
        f()


def _ret_mixer(x, norm_g, cos, sin, w_in, w_out, layer, j, seq_len):
    n, d = x.shape
    c = RET_CHUNK
    tm = RET_TILE
    width = w_in.shape[-1]
    half = RET_QK_DIM // 2
    tiles_per_seq = seq_len // tm
    rows = lambda s: (s, 0)
    pos = lambda s: (lax.rem(s, tiles_per_seq), 0)
    whole = lambda s: (layer, 0, 0)
    wj = lambda s: (j, 0, 0)
    weights = (d * width + RET_HEADS * RET_V_DIM * d) * 2
    streamed = 2 * tm * d * 4 + 2 * tm * half * 4
    temps = (2 * c * width * 4 + RET_HEADS * RET_QK_DIM * RET_V_DIM * 4
             + 2 * c * RET_HEADS * RET_V_DIM * 2 + 8 * c * d * 4
             + RET_HEADS * c * (c + 2 * RET_QK_DIM) * 4)
    return pl.pallas_call(
        functools.partial(_ret_kernel, tiles_per_seq=tiles_per_seq),
        grid=(n // tm,),
        in_specs=[
            pl.BlockSpec((tm, d), rows),
            pl.BlockSpec((tm, half), pos),
            pl.BlockSpec((tm, half), pos),
            _resident((None, N_NORMS, d), whole),
            _resident((None, d, width), wj),
            _resident((None, RET_HEADS * RET_V_DIM, d), wj),
        ],
        out_specs=pl.BlockSpec((tm, d), rows),
        out_shape=jax.ShapeDtypeStruct((n, d), F32),
        scratch_shapes=[pltpu.VMEM((RET_HEADS, RET_QK_DIM, RET_V_DIM), F32),
                        pltpu.VMEM((RET_HEADS, c, c), F32),
                        pltpu.VMEM((RET_HEADS, c, RET_QK_DIM), F32),
                        pltpu.VMEM((RET_HEADS, c, RET_QK_DIM), F32)],
        compiler_params=pltpu.CompilerParams(
            dimension_semantics=("arbitrary",),
            vmem_limit_bytes=_vmem_limit(weights, streamed, temps)),
        name=f"ret_l{layer}",
    )(x, cos, sin, norm_g, w_in, w_out)


def _rope_tables(t):
    half = RET_QK_DIM // 2
    inv_freq = 1.0 / (ROPE_BASE ** jnp.linspace(0.0, 1.0, half, dtype=F32))
    ang = jnp.arange(t).astype(F32)[:, None] * inv_freq[None, :]
    return jnp.cos(ang), jnp.sin(ang)


def kernel(x, p, norm_g, ffn1_w_gate, ffn1_w_up, ffn1_w_down, ffn2_w_gate, ffn2_w_up, ffn2_w_down,
           conv_w_in, conv_w, conv_w_out, ret_w_in, ret_w_out, ple_w_proj, ple_w_gate):
    depth, bsz, t, _ = p.shape
    d = x.shape[-1]
    n = bsz * t
    bf = lambda w: w.astype(BF16)
    ffn1 = (bf(ffn1_w_gate), bf(ffn1_w_up), bf(ffn1_w_down))
    ffn2 = (bf(ffn2_w_gate), bf(ffn2_w_up), bf(ffn2_w_down))
    conv_in, conv_out = bf(conv_w_in), bf(conv_w_out)
    ret_in, ret_out = bf(ret_w_in), bf(ret_w_out)
    ple_proj, ple_gate = bf(ple_w_proj), bf(ple_w_gate)
    p2 = p.reshape(depth, n, D_PLE)
    cos, sin = _rope_tables(t)

    x = x.reshape(n, d)
    for i in range(depth):
        j = i // N_MIXERS
        x = _ffn(x, norm_g, *ffn1, i, FFN1_PRE, FFN1_POST)
        if i % N_MIXERS == 0:
            x = _conv_mixer(x, norm_g, conv_in, conv_w, conv_out, i, j, t)
        else:
            x = _ret_mixer(x, norm_g, cos, sin, ret_in, ret_out, i, j, t)
        x = _ffn(x, norm_g, *ffn2, i, FFN2_PRE, FFN2_POST, ple=(p2, ple_proj, ple_gate))
    return x.reshape(bsz, t, d)
```

```python
import functools
import math

import jax
import jax.numpy as jnp
from jax import lax
from jax.experimental import pallas as pl
from jax.experimental.pallas import tpu as pltpu

D_MODEL = 1024
D_FF = 2816
D_PLE = 256
CONV_WIDTH = 3
RET_HEADS = 4
RET_QK_DIM = D_MODEL // RET_HEADS
RET_V_DIM = 2 * RET_QK_DIM
ROPE_BASE = 10000.0
NORM_EPS = 1e-6
N_MIXERS = 2
N_NORMS = 8
FFN1_PRE, FFN1_POST, MIX_PRE, MIX_POST, FFN2_PRE, FFN2_POST, PLE_PRE, PLE_POST = range(N_NORMS)

V7X_VMEM_BYTES = 64 * 1024 * 1024
V7X_SUBLANES = 8
MXU_DEPTH = 256

FFN_TILE = 1024
FFN_SUBTILE_ROWS = (256, 256, 256, 256)
PLE_ROWS = 256
FFN_COL_CHUNK = 1024
CAST_STEPS = 16
CONV_TILE = 2048
CONV_SUBTILES = 4
RET_CHUNK = 256
RET_TILE = 512

F32 = jnp.float32
BF16 = jnp.bfloat16


def _rmsnorm(x, g):
    ms = jnp.mean(x * x, axis=-1, keepdims=True)
    return x * lax.rsqrt(ms + NORM_EPS) * g


def _silu(x):
    return x * jax.nn.sigmoid(x)


def _dot(a, b):
    return jnp.dot(a, b, preferred_element_type=F32)


def _vmem_limit(resident_bytes, streamed_bytes, temp_bytes):
    need = resident_bytes + 2 * streamed_bytes + temp_bytes
    return min(int(need * 1.25) + (4 << 20), V7X_VMEM_BYTES - (4 << 20))


def _resident(block_shape, index_map):
    return pl.BlockSpec(block_shape, index_map, pipeline_mode=pl.Buffered(1))


def _row_chunks(w, index):
    _, rows, cols = w.shape
    return pl.BlockSpec((None, rows // CAST_STEPS, cols),
                        lambda i: (index, jnp.minimum(i, CAST_STEPS - 1), 0))


def _bf16_copy(w):
    return pltpu.VMEM(w.shape[1:], BF16)


def _cast_rows(i, pairs):
    for src_ref, dst_ref in pairs:
        kc = src_ref.shape[0]
        dst_ref[pl.ds(pl.multiple_of(i * kc, kc), kc), :] = src_ref[...].astype(BF16)


def _ffn_kernel(*refs, pre, post, with_ple):
    if with_ple:
        (x_ref, g_ref, wg32, wu32, wd32, p_ref, wpp32, wpg32, o_ref,
         wg_ref, wu_ref, wd_ref, wpp_ref, wpg_ref) = refs
        casts = [(wg32, wg_ref), (wu32, wu_ref), (wd32, wd_ref), (wpp32, wpp_ref), (wpg32, wpg_ref)]
    else:
        x_ref, g_ref, wg32, wu32, wd32, o_ref, wg_ref, wu_ref, wd_ref = refs
        casts = [(wg32, wg_ref), (wu32, wu_ref), (wd32, wd_ref)]
    i = pl.program_id(0)

    @pl.when(i < CAST_STEPS)
    def _():
        _cast_rows(i, casts)

    def gain(k):
        return g_ref[k:k + 1, :]

    def half_step(lo, hi):
        x = x_ref[lo:hi, :]
        xn = _rmsnorm(x, gain(pre)).astype(BF16)
        y = None
        for a in range(0, D_FF, FFN_COL_CHUNK):
            b = min(a + FFN_COL_CHUNK, D_FF)
            h = (_silu(_dot(xn, wg_ref[:, a:b])) * _dot(xn, wu_ref[:, a:b])).astype(BF16)
            part = _dot(h, wd_ref[a:b, :])
            y = part if y is None else y + part
        return x + 0.5 * _rmsnorm(y, gain(post))

    def embed(lo, hi, x1):
        p = p_ref[lo:hi, :].astype(BF16)
        gate = jax.nn.sigmoid(_dot(_rmsnorm(x1, gain(PLE_PRE)).astype(BF16), wpg_ref[...]))
        return x1 + _rmsnorm(_dot(p, wpp_ref[...]) * gate, gain(PLE_POST))

    @pl.when(i >= CAST_STEPS)
    def _():
        bounds = [sum(FFN_SUBTILE_ROWS[:k]) for k in range(len(FFN_SUBTILE_ROWS) + 1)]
        spans = list(zip(bounds[:-1], bounds[1:]))
        x1 = [half_step(lo, hi) for lo, hi in spans]
        for (lo, hi), v in zip(spans, x1):
            if not with_ple:
                o_ref[lo:hi, :] = v
                continue
            for a in range(lo, hi, PLE_ROWS):
                o_ref[a:a + PLE_ROWS, :] = embed(a, a + PLE_ROWS, v[a - lo:a - lo + PLE_ROWS, :])


def _ffn(x, norm_g, w_gate, w_up, w_down, layer, pre, post, ple=None):
    n, d = x.shape
    tm = FFN_TILE
    assert sum(FFN_SUBTILE_ROWS) == tm and all(r % PLE_ROWS == 0 for r in FFN_SUBTILE_ROWS)
    rows = lambda i: (jnp.maximum(i - CAST_STEPS, 0), 0)
    whole = lambda i: (layer, 0, 0)
    casted = [w_gate, w_up, w_down]
    streamed = 2 * tm * d * 4
    temps = tm * (FFN_COL_CHUNK * (4 + 4 + 2) + 3 * d * 4)
    in_specs = [pl.BlockSpec((tm, d), rows), _resident((None, N_NORMS, d), whole)]
    in_specs += [_row_chunks(w, layer) for w in casted]
    operands = [x, norm_g] + casted
    if ple is not None:
        p, w_proj, w_pgate = ple
        in_specs += [
            pl.BlockSpec((None, tm, D_PLE),
                         lambda i: (layer, jnp.maximum(i - CAST_STEPS, 0), 0)),
            _row_chunks(w_proj, layer),
            _row_chunks(w_pgate, layer),
        ]
        operands += [p, w_proj, w_pgate]
        casted += [w_proj, w_pgate]
        streamed += tm * D_PLE * 4
        temps += tm * 3 * d * 4
    weights = sum(w.shape[1] * w.shape[2] * 2 for w in casted)
    streamed += sum(w.shape[1] * w.shape[2] * 4 for w in casted) // CAST_STEPS
    return pl.pallas_call(
        functools.partial(_ffn_kernel, pre=pre, post=post, with_ple=ple is not None),
        grid=(CAST_STEPS + n // tm,),
        in_specs=in_specs,
        out_specs=pl.BlockSpec((tm, d), rows),
        out_shape=jax.ShapeDtypeStruct((n, d), F32),
        scratch_shapes=[_bf16_copy(w) for w in casted],
        compiler_params=pltpu.CompilerParams(
            dimension_semantics=("arbitrary",),
            vmem_limit_bytes=_vmem_limit(weights, streamed, temps)),
        name=f"ffn_l{layer}_n{pre}",
    )(*operands)


def _conv_kernel(x_ref, g_ref, win32, cw_ref, wout32, o_ref, win_ref, wout_ref, tail_ref,
                 *, tiles_per_seq):
    tm, d = x_ref.shape
    sub = tm // CONV_SUBTILES
    starts = range(0, d, MXU_DEPTH)
    i = pl.program_id(0)

    @pl.when(i < CAST_STEPS)
    def _():
        _cast_rows(i, [(win32, win_ref), (wout32, wout_ref)])

    @pl.when(lax.rem(i - CAST_STEPS, tiles_per_seq) == 0)
    def _():
        tail_ref[...] = jnp.zeros_like(tail_ref)

    @pl.when(i >= CAST_STEPS)
    def _():
        row = lax.broadcasted_iota(jnp.int32, (sub, 1), 0)
        tails = [tail_ref[:, a:a + MXU_DEPTH] for a in starts]

        def close(r, x, bv):
            y = _dot(jnp.concatenate(bv, axis=-1), wout_ref[...])
            o_ref[r * sub:(r + 1) * sub, :] = x + _rmsnorm(y, g_ref[MIX_POST:MIX_POST + 1, :])

        pending = None
        for r in range(CONV_SUBTILES):
            x = x_ref[r * sub:(r + 1) * sub, :]
            xn = _rmsnorm(x, g_ref[MIX_PRE:MIX_PRE + 1, :]).astype(BF16)
            bv = []
            for gi, a in enumerate(starts):
                if gi == 1 and pending is not None:
                    close(*pending)
                cols = slice(a, a + MXU_DEPTH)
                b = _dot(xn, win_ref[:, a:a + MXU_DEPTH])
                u = (_dot(xn, win_ref[:, d + a:d + a + MXU_DEPTH])
                     * _dot(xn, win_ref[:, 2 * d + a:2 * d + a + MXU_DEPTH]))
                last1 = tails[gi][V7X_SUBLANES - 1:, :]
                last2 = tails[gi][V7X_SUBLANES - 2:V7X_SUBLANES - 1, :]
                u1 = jnp.where(row == 0, last1, pltpu.roll(u, 1, axis=0))
                u2 = jnp.where(row == 0, last2, jnp.where(row == 1, last1, pltpu.roll(u, 2, axis=0)))
                tails[gi] = u[sub - V7X_SUBLANES:, :]
                v = cw_ref[0:1, cols] * u2 + cw_ref[1:2, cols] * u1 + cw_ref[2:3, cols] * u
                bv.append((b * v).astype(BF16))
            pending = (r, x, bv)
        close(*pending)
        for gi, a in enumerate(starts):
            tail_ref[:, a:a + MXU_DEPTH] = tails[gi]


def _conv_mixer(x, norm_g, w_in, conv_w, w_out, layer, j, seq_len):
    n, d = x.shape
    tm = CONV_TILE
    rows = lambda s: (jnp.maximum(s - CAST_STEPS, 0), 0)
    whole = lambda s: (layer, 0, 0)
    wj = lambda s: (j, 0, 0)
    weights = 4 * d * d * 2
    streamed = 2 * tm * d * 4 + 4 * d * d * 4 // CAST_STEPS
    temps = tm * d * (4 + 2) + 12 * tm * MXU_DEPTH * 4 + 2 * tm * d * 4
    return pl.pallas_call(
        functools.partial(_conv_kernel, tiles_per_seq=seq_len // tm),
        grid=(CAST_STEPS + n // tm,),
        in_specs=[
            pl.BlockSpec((tm, d), rows),
            _resident((None, N_NORMS, d), whole),
            _row_chunks(w_in, j),
            _resident((None, CONV_WIDTH, d), wj),
            _row_chunks(w_out, j),
        ],
        out_specs=pl.BlockSpec((tm, d), rows),
        out_shape=jax.ShapeDtypeStruct((n, d), F32),
        scratch_shapes=[_bf16_copy(w_in), _bf16_copy(w_out), pltpu.VMEM((V7X_SUBLANES, d), F32)],
        compiler_params=pltpu.CompilerParams(
            dimension_semantics=("arbitrary",),
            vmem_limit_bytes=_vmem_limit(weights, streamed, temps)),
        name=f"conv_l{layer}",
    )(x, norm_g, w_in, conv_w, w_out)


def _rotate(a, cos, sin):
    half = a.shape[-1] // 2
    a1, a2 = a[:, :half], a[:, half:]
    return jnp.concatenate([a1 * cos - a2 * sin, a2 * cos + a1 * sin], axis=-1)


def _ret_kernel(x_ref, cos_ref, sin_ref, g_ref, win32, wout32, o_ref,
                win_ref, wout_ref, state_ref, intra_ref, qdec_ref, kdec_ref, *, tiles_per_seq):
    c = RET_CHUNK
    nh, dk, dv = RET_HEADS, RET_QK_DIM, RET_V_DIM
    n_chunks = x_ref.shape[0] // c
    i = pl.program_id(0)
    s = i - CAST_STEPS

    @pl.when(i < CAST_STEPS)
    def _():
        _cast_rows(i, [(win32, win_ref), (wout32, wout_ref)])

    @pl.when(s == 0)
    def _():
        k_scale = dk ** -0.5
        ri = lax.broadcasted_iota(jnp.int32, (c, c), 0)
        ci = lax.broadcasted_iota(jnp.int32, (c, c), 1)
        diff = (ri - ci).astype(F32)
        idx = lax.broadcasted_iota(jnp.int32, (c, dk), 0).astype(F32)
        for h in range(nh):
            log_gamma = math.log1p(-(2.0 ** (-5.0 - h)))
            intra_ref[h] = jnp.where(
                diff >= 0.0, jnp.exp(log_gamma * jnp.maximum(diff, 0.0)), 0.0) * k_scale
            qdec_ref[h] = jnp.exp(log_gamma * (idx + 1.0))
            kdec_ref[h] = jnp.exp(log_gamma * (c - 1.0 - idx)) * k_scale

    @pl.when(lax.rem(s, tiles_per_seq) == 0)
    def _():
        state_ref[...] = jnp.zeros_like(state_ref)

    k_off, v_off, g_off = nh * dk, 2 * nh * dk, 2 * nh * dk + nh * dv

    def rows(r):
        return slice(r * c, (r + 1) * c)

    def project(xn, h):
        return (_dot(xn, win_ref[:, h * dk:(h + 1) * dk]),
                _dot(xn, win_ref[:, k_off + h * dk:k_off + (h + 1) * dk]),
                _dot(xn, win_ref[:, v_off + h * dv:v_off + (h + 1) * dv]),
                _dot(xn, win_ref[:, g_off + h * dv:g_off + (h + 1) * dv]))

    def retain(r, h, q, k, v, gt):
        chunk_decay = math.exp(math.log1p(-(2.0 ** (-5.0 - h))) * c)
        cos = cos_ref[rows(r), :]
        sin = sin_ref[rows(r), :]
        q = _rotate(q, cos, sin)
        k = _rotate(k, cos, sin)
        v = v.astype(BF16)
        state = state_ref[h]
        scores = lax.dot_general(q.astype(BF16), k.astype(BF16), (((1,), (1,)), ((), ())),
                                 preferred_element_type=F32) * intra_ref[h]
        inner = _dot(scores.astype(BF16), v)
        cross = _dot((q * qdec_ref[h]).astype(BF16), state.astype(BF16))
        kd_t = (k * kdec_ref[h]).T.astype(BF16)
        state_ref[h] = chunk_decay * state + _dot(kd_t, v)
        out = inner + cross
        out = out * lax.rsqrt(jnp.mean(out * out, axis=-1, keepdims=True) + NORM_EPS)
        return (_silu(gt) * out).astype(BF16)

    def close(r, x, ys):
        y = _dot(jnp.concatenate(ys, axis=-1), wout_ref[...])
        o_ref[rows(r), :] = x + _rmsnorm(y, g_ref[MIX_POST:MIX_POST + 1, :])

    @pl.when(i >= CAST_STEPS)
    def _():
        deferred = []
        for r in range(n_chunks):
            x = x_ref[rows(r), :]
            xn = _rmsnorm(x, g_ref[MIX_PRE:MIX_PRE + 1, :]).astype(BF16)
            ys = [None] * nh
            proj = {0: project(xn, 0)}

            def run_retain(h, r=r, ys=ys, proj=proj):
                ys[h] = retain(r, h, *proj.pop(h))

            for h in range(nh):
                if h + 1 < nh:
                    proj[h + 1] = project(xn, h + 1)
                if deferred:
                    deferred.pop(0)()
                if h + 1 < nh:
                    run_retain(h)
            while deferred:
                deferred.pop(0)()
            deferred = [functools.partial(run_retain, nh - 1),
                        functools.partial(close, r, x, ys)]
        for f in deferred:
            f()


def _ret_mixer(x, norm_g, cos, sin, w_in, w_out, layer, j, seq_len):
    n, d = x.shape
    c = RET_CHUNK
    tm = RET_TILE
    width = w_in.shape[-1]
    half = RET_QK_DIM // 2
    tiles_per_seq = seq_len // tm
    rows = lambda s: (jnp.maximum(s - CAST_STEPS, 0), 0)
    pos = lambda s: (lax.rem(jnp.maximum(s - CAST_STEPS, 0), tiles_per_seq), 0)
    whole = lambda s: (layer, 0, 0)
    weights = (d * width + RET_HEADS * RET_V_DIM * d) * 2
    streamed = 2 * tm * d * 4 + 2 * tm * half * 4 + 2 * weights // CAST_STEPS
    temps = (2 * c * width * 4 + RET_HEADS * RET_QK_DIM * RET_V_DIM * 4
             + 2 * c * RET_HEADS * RET_V_DIM * 2 + 8 * c * d * 4
             + RET_HEADS * c * (c + 2 * RET_QK_DIM) * 4)
    return pl.pallas_call(
        functools.partial(_ret_kernel, tiles_per_seq=tiles_per_seq),
        grid=(CAST_STEPS + n // tm,),
        in_specs=[
            pl.BlockSpec((tm, d), rows),
            pl.BlockSpec((tm, half), pos),
            pl.BlockSpec((tm, half), pos),
            _resident((None, N_NORMS, d), whole),
            _row_chunks(w_in, j),
            _row_chunks(w_out, j),
        ],
        out_specs=pl.BlockSpec((tm, d), rows),
        out_shape=jax.ShapeDtypeStruct((n, d), F32),
        scratch_shapes=[_bf16_copy(w_in), _bf16_copy(w_out),
                        pltpu.VMEM((RET_HEADS, RET_QK_DIM, RET_V_DIM), F32),
                        pltpu.VMEM((RET_HEADS, c, c), F32),
                        pltpu.VMEM((RET_HEADS, c, RET_QK_DIM), F32),
                        pltpu.VMEM((RET_HEADS, c, RET_QK_DIM), F32)],
        compiler_params=pltpu.CompilerParams(
            dimension_semantics=("arbitrary",),
            vmem_limit_bytes=_vmem_limit(weights, streamed, temps)),
        name=f"ret_l{layer}",
    )(x, cos, sin, norm_g, w_in, w_out)


def _rope_tables(t):
    half = RET_QK_DIM // 2
    inv_freq = 1.0 / (ROPE_BASE ** jnp.linspace(0.0, 1.0, half, dtype=F32))
    ang = jnp.arange(t).astype(F32)[:, None] * inv_freq[None, :]
    return jnp.cos(ang), jnp.sin(ang)


def kernel(x, p, norm_g, ffn1_w_gate, ffn1_w_up, ffn1_w_down, ffn2_w_gate, ffn2_w_up, ffn2_w_down,
           conv_w_in, conv_w, conv_w_out, ret_w_in, ret_w_out, ple_w_proj, ple_w_gate):
    depth, bsz, t, _ = p.shape
    d = x.shape[-1]
    n = bsz * t
    ffn1 = (ffn1_w_gate, ffn1_w_up, ffn1_w_down)
    ffn2 = (ffn2_w_gate, ffn2_w_up, ffn2_w_down)
    p2 = p.reshape(depth, n, D_PLE)
    cos, sin = _rope_tables(t)

    x = x.reshape(n, d)
    for i in range(depth):
        j = i // N_MIXERS
        x = _ffn(x, norm_g, *ffn1, i, FFN1_PRE, FFN1_POST)
        if i % N_MIXERS == 0:
            x = _conv_mixer(x, norm_g, conv_w_in, conv_w, conv_w_out, i, j, t)
        else:
            x = _ret_mixer(x, norm_g, cos, sin, ret_w_in, ret_w_out, i, j, t)
        x = _ffn(x, norm_g, *ffn2, i, FFN2_PRE, FFN2_POST, ple=(p2, ple_w_proj, ple_w_gate))
    return x.reshape(bsz, t, d)
```

```python
import functools
import math

import jax
import jax.numpy as jnp
from jax import lax
from jax.experimental import pallas as pl
from jax.experimental.pallas import tpu as pltpu

D_MODEL = 1024
D_FF = 2816
D_PLE = 256
CONV_WIDTH = 3
RET_HEADS = 4
RET_QK_DIM = D_MODEL // RET_HEADS
RET_V_DIM = 2 * RET_QK_DIM
ROPE_BASE = 10000.0
NORM_EPS = 1e-6
N_MIXERS = 2
N_NORMS = 8
FFN1_PRE, FFN1_POST, MIX_PRE, MIX_POST, FFN2_PRE, FFN2_POST, PLE_PRE, PLE_POST = range(N_NORMS)

V7X_VMEM_BYTES = 64 * 1024 * 1024
V7X_VMEM_RESERVE = 4 * 1024 * 1024
V7X_SUBLANES = 8
BF16_ROWS_PER_TILE = 16
MXU_DEPTH = 256

FFN_TILE = 1024
FFN_SUBTILE_ROWS = (256, 256, 256, 256)
PLE_ROWS = 256
FFN_COL_CHUNK = 1024
CAST_STEPS = 16
CONV_TILE = 1024
CONV_SUBTILES = 2
RET_CHUNK = 256
RET_SEQS_PER_STEP = 2

F32 = jnp.float32
BF16 = jnp.bfloat16


def _rmsnorm(x, g):
    ms = jnp.mean(x * x, axis=-1, keepdims=True)
    return x * lax.rsqrt(ms + NORM_EPS) * g


def _silu(x):
    return x * jax.nn.sigmoid(x)


def _dot(a, b):
    return jnp.dot(a, b, preferred_element_type=F32)


def _vmem_limit(resident_bytes, streamed_bytes, temp_bytes):
    need = resident_bytes + 2 * streamed_bytes + temp_bytes + V7X_VMEM_RESERVE
    assert need <= V7X_VMEM_BYTES, need
    return need


def _resident(block_shape, index_map):
    return pl.BlockSpec(block_shape, index_map, pipeline_mode=pl.Buffered(1))


def _row_chunks(w, index):
    _, rows, cols = w.shape
    assert rows % (CAST_STEPS * BF16_ROWS_PER_TILE) == 0, w.shape
    return pl.BlockSpec((None, rows // CAST_STEPS, cols),
                        lambda i: (index, jnp.minimum(i, CAST_STEPS - 1), 0))


def _bf16_copy(w):
    return pltpu.VMEM(w.shape[1:], BF16)


def _cast_rows(i, pairs):
    for src_ref, dst_ref in pairs:
        kc = src_ref.shape[0]
        dst_ref[pl.ds(pl.multiple_of(i * kc, kc), kc), :] = src_ref[...].astype(BF16)


def _ffn_kernel(*refs, pre, post, with_ple):
    if with_ple:
        (x_ref, g_ref, wg32, wu32, wd32, p_ref, wpp32, wpg32, o_ref,
         wg_ref, wu_ref, wd_ref, wpp_ref, wpg_ref) = refs
        casts = [(wg32, wg_ref), (wu32, wu_ref), (wd32, wd_ref), (wpp32, wpp_ref), (wpg32, wpg_ref)]
    else:
        x_ref, g_ref, wg32, wu32, wd32, o_ref, wg_ref, wu_ref, wd_ref = refs
        casts = [(wg32, wg_ref), (wu32, wu_ref), (wd32, wd_ref)]
    i = pl.program_id(0)

    @pl.when(i < CAST_STEPS)
    def _():
        _cast_rows(i, casts)

    def gain(k):
        return g_ref[k:k + 1, :]

    def half_step(lo, hi):
        x = x_ref[lo:hi, :]
        xn = _rmsnorm(x, gain(pre)).astype(BF16)
        y = None
        for a in range(0, D_FF, FFN_COL_CHUNK):
            b = min(a + FFN_COL_CHUNK, D_FF)
            h = (_silu(_dot(xn, wg_ref[:, a:b])) * _dot(xn, wu_ref[:, a:b])).astype(BF16)
            part = _dot(h, wd_ref[a:b, :])
            y = part if y is None else y + part
        return x + 0.5 * _rmsnorm(y, gain(post))

    def embed(lo, hi, x1):
        p = p_ref[lo:hi, :].astype(BF16)
        gate = jax.nn.sigmoid(_dot(_rmsnorm(x1, gain(PLE_PRE)).astype(BF16), wpg_ref[...]))
        return x1 + _rmsnorm(_dot(p, wpp_ref[...]) * gate, gain(PLE_POST))

    @pl.when(i >= CAST_STEPS)
    def _():
        bounds = [sum(FFN_SUBTILE_ROWS[:k]) for k in range(len(FFN_SUBTILE_ROWS) + 1)]
        spans = list(zip(bounds[:-1], bounds[1:]))
        x1 = [half_step(lo, hi) for lo, hi in spans]
        for (lo, hi), v in zip(spans, x1):
            if not with_ple:
                o_ref[lo:hi, :] = v
                continue
            for a in range(lo, hi, PLE_ROWS):
                o_ref[a:a + PLE_ROWS, :] = embed(a, a + PLE_ROWS, v[a - lo:a - lo + PLE_ROWS, :])


def _ffn(x, norm_g, w_gate, w_up, w_down, layer, pre, post, ple=None):
    n, d = x.shape
    tm = FFN_TILE
    assert sum(FFN_SUBTILE_ROWS) == tm and all(r % PLE_ROWS == 0 for r in FFN_SUBTILE_ROWS)
    rows = lambda i: (jnp.maximum(i - CAST_STEPS, 0), 0)
    whole = lambda i: (layer, 0, 0)
    casted = [w_gate, w_up, w_down]
    streamed = 2 * tm * d * 4
    in_flight = 2 * max(FFN_SUBTILE_ROWS)
    temps = in_flight * (FFN_COL_CHUNK * (4 + 4 + 2) + 3 * d * 4)
    in_specs = [pl.BlockSpec((tm, d), rows), _resident((None, N_NORMS, d), whole)]
    in_specs += [_row_chunks(w, layer) for w in casted]
    operands = [x, norm_g] + casted
    if ple is not None:
        p, w_proj, w_pgate = ple
        in_specs += [
            pl.BlockSpec((None, tm, D_PLE),
                         lambda i: (layer, jnp.maximum(i - CAST_STEPS, 0), 0)),
            _row_chunks(w_proj, layer),
            _row_chunks(w_pgate, layer),
        ]
        operands += [p, w_proj, w_pgate]
        casted += [w_proj, w_pgate]
        streamed += tm * D_PLE * 4
        temps += in_flight * 3 * d * 4
    weights = sum(w.shape[1] * w.shape[2] * 2 for w in casted)
    streamed += sum(w.shape[1] * w.shape[2] * 4 for w in casted) // CAST_STEPS
    return pl.pallas_call(
        functools.partial(_ffn_kernel, pre=pre, post=post, with_ple=ple is not None),
        grid=(CAST_STEPS + n // tm,),
        in_specs=in_specs,
        out_specs=pl.BlockSpec((tm, d), rows),
        out_shape=jax.ShapeDtypeStruct((n, d), F32),
        scratch_shapes=[_bf16_copy(w) for w in casted],
        compiler_params=pltpu.CompilerParams(
            dimension_semantics=("arbitrary",),
            vmem_limit_bytes=_vmem_limit(weights, streamed, temps)),
        name=f"ffn_l{layer}_n{pre}",
    )(*operands)


def _conv_kernel(x_ref, g_ref, win32, cw_ref, wout32, o_ref, win_ref, wout_ref, tail_ref,
                 *, tiles_per_seq):
    tm, d = x_ref.shape
    sub = tm // CONV_SUBTILES
    starts = range(0, d, MXU_DEPTH)
    i = pl.program_id(0)

    @pl.when(i < CAST_STEPS)
    def _():
        _cast_rows(i, [(win32, win_ref), (wout32, wout_ref)])

    @pl.when(lax.rem(i - CAST_STEPS, tiles_per_seq) == 0)
    def _():
        tail_ref[...] = jnp.zeros_like(tail_ref)

    @pl.when(i >= CAST_STEPS)
    def _():
        row = lax.broadcasted_iota(jnp.int32, (sub, 1), 0)
        tails = [tail_ref[:, a:a + MXU_DEPTH] for a in starts]

        def close(r, x, bv):
            y = _dot(jnp.concatenate(bv, axis=-1), wout_ref[...])
            o_ref[r * sub:(r + 1) * sub, :] = x + _rmsnorm(y, g_ref[MIX_POST:MIX_POST + 1, :])

        pending = None
        for r in range(CONV_SUBTILES):
            x = x_ref[r * sub:(r + 1) * sub, :]
            xn = _rmsnorm(x, g_ref[MIX_PRE:MIX_PRE + 1, :]).astype(BF16)
            bv = []
            for gi, a in enumerate(starts):
                if gi == 1 and pending is not None:
                    close(*pending)
                cols = slice(a, a + MXU_DEPTH)
                b = _dot(xn, win_ref[:, a:a + MXU_DEPTH])
                u = (_dot(xn, win_ref[:, d + a:d + a + MXU_DEPTH])
                     * _dot(xn, win_ref[:, 2 * d + a:2 * d + a + MXU_DEPTH]))
                last1 = tails[gi][V7X_SUBLANES - 1:, :]
                last2 = tails[gi][V7X_SUBLANES - 2:V7X_SUBLANES - 1, :]
                u1 = jnp.where(row == 0, last1, pltpu.roll(u, 1, axis=0))
                u2 = jnp.where(row == 0, last2, jnp.where(row == 1, last1, pltpu.roll(u, 2, axis=0)))
                tails[gi] = u[sub - V7X_SUBLANES:, :]
                v = cw_ref[0:1, cols] * u2 + cw_ref[1:2, cols] * u1 + cw_ref[2:3, cols] * u
                bv.append((b * v).astype(BF16))
            pending = (r, x, bv)
        close(*pending)
        for gi, a in enumerate(starts):
            tail_ref[:, a:a + MXU_DEPTH] = tails[gi]


def _conv_mixer(x, norm_g, w_in, conv_w, w_out, layer, j, seq_len):
    n, d = x.shape
    tm = CONV_TILE
    rows = lambda s: (jnp.maximum(s - CAST_STEPS, 0), 0)
    whole = lambda s: (layer, 0, 0)
    wj = lambda s: (j, 0, 0)
    weights = 4 * d * d * 2
    streamed = 2 * tm * d * 4 + 4 * d * d * 4 // CAST_STEPS
    temps = tm * (d * (4 + 2 + 2 + 4) + 8 * MXU_DEPTH * 4)
    return pl.pallas_call(
        functools.partial(_conv_kernel, tiles_per_seq=seq_len // tm),
        grid=(CAST_STEPS + n // tm,),
        in_specs=[
            pl.BlockSpec((tm, d), rows),
            _resident((None, N_NORMS, d), whole),
            _row_chunks(w_in, j),
            _resident((None, CONV_WIDTH, d), wj),
            _row_chunks(w_out, j),
        ],
        out_specs=pl.BlockSpec((tm, d), rows),
        out_shape=jax.ShapeDtypeStruct((n, d), F32),
        scratch_shapes=[_bf16_copy(w_in), _bf16_copy(w_out), pltpu.VMEM((V7X_SUBLANES, d), F32)],
        compiler_params=pltpu.CompilerParams(
            dimension_semantics=("arbitrary",),
            vmem_limit_bytes=_vmem_limit(weights, streamed, temps)),
        name=f"conv_l{layer}",
    )(x, norm_g, w_in, conv_w, w_out)


def _rotate(a, cos, sin):
    half = a.shape[-1] // 2
    a1, a2 = a[:, :half], a[:, half:]
    return jnp.concatenate([a1 * cos - a2 * sin, a2 * cos + a1 * sin], axis=-1)


def _ret_kernel(x_ref, cos_ref, sin_ref, g_ref, win32, wout32, o_ref,
                win_ref, wout_ref, state_ref, intra_ref, qdec_ref, kdec_ref, *, tiles_per_seq):
    n_seq, c, _ = x_ref.shape
    nh, dk, dv = RET_HEADS, RET_QK_DIM, RET_V_DIM
    i = pl.program_id(0)
    s = i - CAST_STEPS

    @pl.when(i < CAST_STEPS)
    def _():
        _cast_rows(i, [(win32, win_ref), (wout32, wout_ref)])

    @pl.when(s == 0)
    def _():
        k_scale = dk ** -0.5
        ri = lax.broadcasted_iota(jnp.int32, (c, c), 0)
        ci = lax.broadcasted_iota(jnp.int32, (c, c), 1)
        diff = (ri - ci).astype(F32)
        idx = lax.broadcasted_iota(jnp.int32, (c, dk), 0).astype(F32)
        for h in range(nh):
            log_gamma = math.log1p(-(2.0 ** (-5.0 - h)))
            intra_ref[h] = jnp.where(
                diff >= 0.0, jnp.exp(log_gamma * jnp.maximum(diff, 0.0)), 0.0) * k_scale
            qdec_ref[h] = jnp.exp(log_gamma * (idx + 1.0))
            kdec_ref[h] = jnp.exp(log_gamma * (c - 1.0 - idx)) * k_scale

    @pl.when(lax.rem(s, tiles_per_seq) == 0)
    def _():
        state_ref[...] = jnp.zeros_like(state_ref)

    k_off, v_off, g_off = nh * dk, 2 * nh * dk, 2 * nh * dk + nh * dv

    def project(xn, h):
        return (_dot(xn, win_ref[:, h * dk:(h + 1) * dk]),
                _dot(xn, win_ref[:, k_off + h * dk:k_off + (h + 1) * dk]),
                _dot(xn, win_ref[:, v_off + h * dv:v_off + (h + 1) * dv]),
                _dot(xn, win_ref[:, g_off + h * dv:g_off + (h + 1) * dv]))

    def retain(r, h, q, k, v, gt):
        chunk_decay = math.exp(math.log1p(-(2.0 ** (-5.0 - h))) * c)
        cos = cos_ref[...]
        sin = sin_ref[...]
        q = _rotate(q, cos, sin)
        k = _rotate(k, cos, sin)
        v = v.astype(BF16)
        state = state_ref[r, h]
        scores = lax.dot_general(q.astype(BF16), k.astype(BF16), (((1,), (1,)), ((), ())),
                                 preferred_element_type=F32) * intra_ref[h]
        inner = _dot(scores.astype(BF16), v)
        cross = _dot((q * qdec_ref[h]).astype(BF16), state.astype(BF16))
        kd_t = (k * kdec_ref[h]).T.astype(BF16)
        state_ref[r, h] = chunk_decay * state + _dot(kd_t, v)
        out = inner + cross
        out = out * lax.rsqrt(jnp.mean(out * out, axis=-1, keepdims=True) + NORM_EPS)
        return (_silu(gt) * out).astype(BF16)

    def close(r, x, ys):
        y = _dot(jnp.concatenate(ys, axis=-1), wout_ref[...])
        o_ref[r] = x + _rmsnorm(y, g_ref[MIX_POST:MIX_POST + 1, :])

    @pl.when(i >= CAST_STEPS)
    def _():
        deferred = []
        for r in range(n_seq):
            x = x_ref[r]
            xn = _rmsnorm(x, g_ref[MIX_PRE:MIX_PRE + 1, :]).astype(BF16)
            ys = [None] * nh
            proj = {0: project(xn, 0)}

            def run_retain(h, r=r, ys=ys, proj=proj):
                ys[h] = retain(r, h, *proj.pop(h))

            for h in range(nh):
                if h + 1 < nh:
                    proj[h + 1] = project(xn, h + 1)
                if deferred:
                    deferred.pop(0)()
                if h + 1 < nh:
                    run_retain(h)
            while deferred:
                deferred.pop(0)()
            deferred = [functools.partial(run_retain, nh - 1),
                        functools.partial(close, r, x, ys)]
        for f in deferred:
            f()


def _ret_mixer(x, norm_g, cos, sin, w_in, w_out, layer, j):
    bsz, t, d = x.shape
    c = RET_CHUNK
    nseq = RET_SEQS_PER_STEP
    width = w_in.shape[-1]
    half = RET_QK_DIM // 2
    tiles_per_seq = t // c
    step = lambda s: jnp.maximum(s - CAST_STEPS, 0)
    rows = lambda s: (step(s) // tiles_per_seq, lax.rem(step(s), tiles_per_seq), 0)
    pos = lambda s: (lax.rem(step(s), tiles_per_seq), 0)
    whole = lambda s: (layer, 0, 0)
    weights = (d * width + RET_HEADS * RET_V_DIM * d) * 2
    state = nseq * RET_HEADS * RET_QK_DIM * RET_V_DIM * 4
    tables = RET_HEADS * c * (c + 2 * RET_QK_DIM) * 4
    streamed = 2 * nseq * c * d * 4 + 2 * c * half * 4 + 2 * weights // CAST_STEPS
    temps = 2 * c * (width // RET_HEADS) * 4 * 2 + nseq * c * RET_HEADS * RET_V_DIM * 2 + 4 * c * d * 4
    return pl.pallas_call(
        functools.partial(_ret_kernel, tiles_per_seq=tiles_per_seq),
        grid=(CAST_STEPS + (bsz // nseq) * tiles_per_seq,),
        in_specs=[
            pl.BlockSpec((nseq, c, d), rows),
            pl.BlockSpec((c, half), pos),
            pl.BlockSpec((c, half), pos),
            _resident((None, N_NORMS, d), whole),
            _row_chunks(w_in, j),
            _row_chunks(w_out, j),
        ],
        out_specs=pl.BlockSpec((nseq, c, d), rows),
        out_shape=jax.ShapeDtypeStruct((bsz, t, d), F32),
        scratch_shapes=[_bf16_copy(w_in), _bf16_copy(w_out),
                        pltpu.VMEM((nseq, RET_HEADS, RET_QK_DIM, RET_V_DIM), F32),
                        pltpu.VMEM((RET_HEADS, c, c), F32),
                        pltpu.VMEM((RET_HEADS, c, RET_QK_DIM), F32),
                        pltpu.VMEM((RET_HEADS, c, RET_QK_DIM), F32)],
        compiler_params=pltpu.CompilerParams(
            dimension_semantics=("arbitrary",),
            vmem_limit_bytes=_vmem_limit(weights + state + tables, streamed, temps)),
        name=f"ret_l{layer}",
    )(x, cos, sin, norm_g, w_in, w_out)


def _rope_tables(t):
    half = RET_QK_DIM // 2
    inv_freq = 1.0 / (ROPE_BASE ** jnp.linspace(0.0, 1.0, half, dtype=F32))
    ang = jnp.arange(t).astype(F32)[:, None] * inv_freq[None, :]
    return jnp.cos(ang), jnp.sin(ang)


def kernel(x, p, norm_g, ffn1_w_gate, ffn1_w_up, ffn1_w_down, ffn2_w_gate, ffn2_w_up, ffn2_w_down,
           conv_w_in, conv_w, conv_w_out, ret_w_in, ret_w_out, ple_w_proj, ple_w_gate):
    depth, bsz, t, _ = p.shape
    d = x.shape[-1]
    n = bsz * t
    ffn1 = (ffn1_w_gate, ffn1_w_up, ffn1_w_down)
    ffn2 = (ffn2_w_gate, ffn2_w_up, ffn2_w_down)
    p2 = p.reshape(depth, n, D_PLE)
    cos, sin = _rope_tables(t)

    x = x.reshape(n, d)
    for i in range(depth):
        j = i // N_MIXERS
        x = _ffn(x, norm_g, *ffn1, i, FFN1_PRE, FFN1_POST)
        if i % N_MIXERS == 0:
            x = _conv_mixer(x, norm_g, conv_w_in, conv_w, conv_w_out, i, j, t)
        else:
            x = _ret_mixer(x.reshape(bsz, t, d), norm_g, cos, sin, ret_w_in, ret_w_out,
                           i, j).reshape(n, d)
        x = _ffn(x, norm_g, *ffn2, i, FFN2_PRE, FFN2_POST, ple=(p2, ple_w_proj, ple_w_gate))
    return x.reshape(bsz, t, d)
```
